```python
import jax
import jax.numpy as jnp
from jax import lax
import numpy as np

D_MODEL = 2048
BATCH = 2
SEQ = 4096
DEPTH = 4

N_A_LAYERS = DEPTH // 2
RET_HEADS = 8
RET_DK = D_MODEL // RET_HEADS
RET_DV = 2 * D_MODEL // RET_HEADS
RET_CHUNK = 128
RET_THETA_BASE = 10000.0
NSA_HEADS = 16
NSA_GROUPS = 4
NSA_HPG = NSA_HEADS // NSA_GROUPS
NSA_DV = D_MODEL // NSA_HEADS
NSA_DK = 3 * NSA_DV // 2
CMP_LEN = 32
CMP_STRIDE = 16
CMP_HID = 2 * NSA_DV
SEL_LEN = 64
SEL_TOPK = 16
WINDOW = 512
SEL_QBLOCK = 32
WIN_QBLOCK = 128
N_BRANCH = 3
ALPHA = (2.0 * DEPTH) ** 0.25
BETA = (8.0 * DEPTH) ** -0.25
NEG_INF = -1e30
FORCE_SCORE = 1e9
LN_EPS = 1e-5

kernel_name = 'hybrid_retnet_nsa_yoco_deepnorm'


def layer_norm(x, g, b):
    xf = x.astype(jnp.float32)
    mu = xf.mean(-1, keepdims=True)
    var = jnp.mean(jnp.square(xf - mu), -1, keepdims=True)
    return ((xf - mu) * lax.rsqrt(var + LN_EPS) * g + b).astype(x.dtype)


def rotate_pairs(t, positions):
    d = t.shape[-1]
    inv_freq = 1.0 / (RET_THETA_BASE ** jnp.linspace(0.0, 1.0, d // 2, dtype=jnp.float32))
    ang = positions.astype(jnp.float32)[:, :, None, None] * inv_freq
    cos, sin = jnp.cos(ang), jnp.sin(ang)
    pair = t.reshape(*t.shape[:-1], d // 2, 2)
    even, odd = pair[..., 0], pair[..., 1]
    return jnp.stack([even * cos - odd * sin, even * sin + odd * cos], axis=-1).reshape(t.shape)


def retention_mixer(x, positions, w_in, w_out):
    bsz, seq, _ = x.shape
    h, dk, dv, c = RET_HEADS, RET_DK, RET_DV, RET_CHUNK
    n_chunk = seq // c
    f32 = jnp.float32
    proj = x @ w_in
    q, k, v, z = jnp.split(proj, [h * dk, 2 * h * dk, 2 * h * dk + h * dv], axis=-1)
    q = rotate_pairs(q.reshape(bsz, seq, h, dk).astype(f32), positions)
    k = rotate_pairs(k.reshape(bsz, seq, h, dk).astype(f32), positions) * (dk ** -0.5)
    v = v.reshape(bsz, seq, h, dv).astype(f32)

    def to_chunks(t):
        return t.reshape(bsz, n_chunk, c, h, -1).transpose(1, 0, 3, 2, 4)

    log_gamma = jnp.log1p(-jnp.exp2(-5.0 - jnp.arange(h, dtype=f32)))
    idx = jnp.arange(c, dtype=f32)
    rel = idx[:, None] - idx[None, :]
    decay_intra = jnp.where(rel >= 0, jnp.exp(log_gamma[:, None, None] * jnp.maximum(rel, 0.0)), 0.0)
    decay_q = jnp.exp(log_gamma[:, None] * (idx + 1.0))[None, :, :, None]
    decay_k = jnp.exp(log_gamma[:, None] * (c - 1.0 - idx))[None, :, :, None]
    decay_chunk = jnp.exp(log_gamma * c)[None, :, None, None]

    def step(state, qkv):
        qc, kc, vc = qkv
        scores = jnp.einsum('bhcd,bhmd->bhcm', qc, kc) * decay_intra
        out = (jnp.einsum('bhcm,bhme->bhce', scores, vc)
               + jnp.einsum('bhcd,bhde->bhce', qc, state) * decay_q)
        state = state * decay_chunk + jnp.einsum('bhcd,bhce->bhde', kc * decay_k, vc)
        return state, out

    state0 = jnp.zeros((bsz, h, dk, dv), f32)
    _, o = lax.scan(step, state0, (to_chunks(q), to_chunks(k), to_chunks(v)))
    o = o.transpose(1, 0, 3, 2, 4).reshape(bsz, seq, h, dv)
    mu = o.mean(-1, keepdims=True)
    var = jnp.mean(jnp.square(o - mu), -1, keepdims=True)
    o = ((o - mu) * lax.rsqrt(var + LN_EPS)).reshape(bsz, seq, h * dv).astype(x.dtype)
    return (o * jax.nn.silu(z)) @ w_out


def nsa_shared_kv(hs, w_kv, pe_k, pe_v, w_ck1, w_ck2, w_cv1, w_cv2):
    bsz, seq, _ = hs.shape
    g, dk, dv = NSA_GROUPS, NSA_DK, NSA_DV
    kv = hs @ w_kv
    sizes = [g * dk, g * dv] * N_BRANCH
    parts = jnp.split(kv, np.cumsum(sizes)[:-1].tolist(), axis=-1)
    k_cmp, v_cmp, k_sel, v_sel, k_win, v_win = [p.reshape(bsz, seq, g, -1).transpose(0, 2, 1, 3) for p in parts]
    n_cmp = (seq - CMP_LEN) // CMP_STRIDE + 1
    tok = jnp.arange(n_cmp)[:, None] * CMP_STRIDE + jnp.arange(CMP_LEN)[None, :]

    def compress(t, pe, w1, w2):
        blocks = t[:, :, tok] + pe
        flat = blocks.reshape(bsz, g, n_cmp, -1)
        return jax.nn.silu(flat @ w1) @ w2

    kc = compress(k_cmp, pe_k, w_ck1, w_ck2)
    vc = compress(v_cmp, pe_v, w_cv1, w_cv2)
    return (kc, vc, k_sel, v_sel, k_win, v_win)


def nsa_mixer(x, shared, w_in, w_out):
    kc, vc, ks, vs, kw, vw = shared
    bsz, seq, _ = x.shape
    g, r, dk, dv, h = NSA_GROUPS, NSA_HPG, NSA_DK, NSA_DV, NSA_HEADS
    f32 = jnp.float32
    scale = dk ** -0.5
    proj = x @ w_in
    q, z_cmp, z_sel, z_win, gate = jnp.split(
        proj, [h * dk, h * dk + h * dv, h * dk + 2 * h * dv, h * dk + 3 * h * dv], axis=-1)
    q = q.reshape(bsz, seq, g, r, dk).transpose(0, 2, 3, 1, 4)
    t = jnp.arange(seq)

    n_cmp = kc.shape[2]
    blk_end = jnp.arange(n_cmp) * CMP_STRIDE + CMP_LEN - 1
    cmask = blk_end[None, :] <= t[:, None]
    s_cmp = jnp.einsum('bgrsd,bgnd->bgrsn', q, kc).astype(f32) * scale
    p_cmp = jax.nn.softmax(jnp.where(cmask, s_cmp, NEG_INF), axis=-1) * cmask.any(-1, keepdims=True).astype(f32)
    o_cmp = jnp.einsum('bgrsn,bgne->bgrse', p_cmp.astype(vc.dtype), vc)

    n_sel = seq // SEL_LEN
    a, b = SEL_LEN // CMP_STRIDE, CMP_LEN // CMP_STRIDE
    span = a + b - 2
    diff = jnp.arange(n_cmp)[:, None] - a * jnp.arange(n_sel)[None, :]
    overlap = jnp.where((diff >= 0) & (diff <= span),
                        jnp.minimum(jnp.minimum(diff, span - diff), min(a, b) - 1) + 1, 0).astype(f32)
    p_sel = jnp.einsum('bgrsn,nj->bgsj', p_cmp, overlap)
    blk = jnp.arange(n_sel)[None, :]
    cur = (t // SEL_LEN)[:, None]
    forced = (blk == 0) | (blk == cur) | (blk == cur - 1)
    score = jnp.where(forced, FORCE_SCORE, jnp.where(blk <= cur, p_sel, -1.0))
    n_top = min(SEL_TOPK, n_sel)
    _, sel_idx = lax.top_k(score, n_top)

    ks_blk = ks.reshape(bsz, g, n_sel, SEL_LEN, dk)
    vs_blk = vs.reshape(bsz, g, n_sel, SEL_LEN, dv)
    n_qb = seq // SEL_QBLOCK
    q_blocks = q.reshape(bsz, g, r, n_qb, SEL_QBLOCK, dk).transpose(3, 0, 1, 2, 4, 5)
    idx_blocks = sel_idx.reshape(bsz, g, n_qb, SEL_QBLOCK, n_top).transpose(2, 0, 1, 3, 4)
    q_starts = jnp.arange(n_qb) * SEL_QBLOCK
    bi = jnp.arange(bsz)[:, None, None, None]
    gi = jnp.arange(g)[None, :, None, None]
    tok_in_blk = jnp.arange(SEL_LEN)

    def sel_block(args):
        qb, ib, q0 = args
        kg = ks_blk[bi, gi, ib].reshape(bsz, g, SEL_QBLOCK, n_top * SEL_LEN, dk)
        vg = vs_blk[bi, gi, ib].reshape(bsz, g, SEL_QBLOCK, n_top * SEL_LEN, dv)
        kpos = (ib[..., None] * SEL_LEN + tok_in_blk).reshape(bsz, g, SEL_QBLOCK, n_top * SEL_LEN)
        qpos = q0 + jnp.arange(SEL_QBLOCK)
        mask = (kpos <= qpos[None, None, :, None])[:, :, None]
        s = jnp.einsum('bgrqd,bgqkd->bgrqk', qb, kg).astype(f32) * scale
        p = jax.nn.softmax(jnp.where(mask, s, NEG_INF), axis=-1)
        return jnp.einsum('bgrqk,bgqke->bgrqe', p.astype(vg.dtype), vg)

    o_sel = lax.map(sel_block, (q_blocks, idx_blocks, q_starts))
    o_sel = o_sel.transpose(1, 2, 3, 0, 4, 5).reshape(bsz, g, r, seq, dv)

    kw_pad = jnp.pad(kw, ((0, 0), (0, 0), (WINDOW, 0), (0, 0)))
    vw_pad = jnp.pad(vw, ((0, 0), (0, 0), (WINDOW, 0), (0, 0)))
    span_k = WIN_QBLOCK + WINDOW

    def win_block(q0):
        qb = lax.dynamic_slice_in_dim(q, q0, WIN_QBLOCK, axis=3)
        kb = lax.dynamic_slice_in_dim(kw_pad, q0, span_k, axis=2)
        vb = lax.dynamic_slice_in_dim(vw_pad, q0, span_k, axis=2)
        qpos = q0 + jnp.arange(WIN_QBLOCK)
        kpos = q0 - WINDOW + jnp.arange(span_k)
        dist = qpos[:, None] - kpos[None, :]
        mask = (dist >= 0) & (dist < WINDOW) & (kpos[None, :] >= 0)
        s = jnp.einsum('bgrqd,bgkd->bgrqk', qb, kb).astype(f32) * scale
        p = jax.nn.softmax(jnp.where(mask, s, NEG_INF), axis=-1)
        return jnp.einsum('bgrqk,bgke->bgrqe', p.astype(vb.dtype), vb)

    o_win = lax.map(win_block, jnp.arange(seq // WIN_QBLOCK) * WIN_QBLOCK)
    o_win = o_win.transpose(1, 2, 3, 0, 4, 5).reshape(bsz, g, r, seq, dv)

    def heads(o):
        return o.transpose(0, 3, 1, 2, 4).reshape(bsz, seq, h, dv).astype(x.dtype)

    def gate_path(zz):
        return jax.nn.silu(zz.reshape(bsz, seq, h, dv))

    gates = jax.nn.sigmoid(gate.reshape(bsz, seq, N_BRANCH, h))[..., None]
    mixed = (gates[:, :, 0] * heads(o_cmp) * gate_path(z_cmp)
             + gates[:, :, 1] * heads(o_sel) * gate_path(z_sel)
             + gates[:, :, 2] * heads(o_win) * gate_path(z_win))
    return mixed.reshape(bsz, seq, h * dv) @ w_out


def setup_inputs(seed: int = 0) -> dict:
    key = jax.random.key(seed)
    keys = iter(jax.random.split(key, 64))
    f32 = jnp.float32

    def dense(fan_in, fan_out, scale=1.0):
        return jax.random.normal(next(keys), (fan_in, fan_out), f32) * (scale * fan_in ** -0.5)

    def norm_pair():
        gain = 1.0 + 0.05 * jax.random.normal(next(keys), (D_MODEL,), f32)
        bias = 0.02 * jax.random.normal(next(keys), (D_MODEL,), f32)
        return gain, bias

    inputs = {}
    inputs['x'] = jax.random.normal(next(keys), (BATCH, SEQ, D_MODEL), f32)
    offset = jax.random.randint(next(keys), (BATCH, 1), 0, 1024, dtype=jnp.int32)
    inputs['positions'] = offset + jnp.arange(SEQ, dtype=jnp.int32)[None, :]
    hr = RET_HEADS
    for layer in range(N_A_LAYERS):
        inputs[f'ret_w_in_{layer}'] = jnp.concatenate([
            dense(D_MODEL, hr * RET_DK), dense(D_MODEL, hr * RET_DK),
            dense(D_MODEL, hr * RET_DV, BETA), dense(D_MODEL, hr * RET_DV)], axis=1)
        inputs[f'ret_w_out_{layer}'] = dense(hr * RET_DV, D_MODEL, BETA)
        gain, bias = norm_pair()
        inputs[f'ln_g_{layer}'] = gain
        inputs[f'ln_b_{layer}'] = bias
    cols = []
    for _ in range(N_BRANCH):
        cols += [dense(D_MODEL, NSA_GROUPS * NSA_DK), dense(D_MODEL, NSA_GROUPS * NSA_DV, BETA)]
    inputs['nsa_w_kv'] = jnp.concatenate(cols, axis=1)
    inputs['nsa_pe_k'] = 0.1 * jax.random.normal(next(keys), (CMP_LEN, NSA_DK), f32)
    inputs['nsa_pe_v'] = 0.1 * jax.random.normal(next(keys), (CMP_LEN, NSA_DV), f32)
    inputs['nsa_w_ck1'] = dense(CMP_LEN * NSA_DK, CMP_HID)
    inputs['nsa_w_ck2'] = dense(CMP_HID, NSA_DK)
    inputs['nsa_w_cv1'] = dense(CMP_LEN * NSA_DV, CMP_HID)
    inputs['nsa_w_cv2'] = dense(CMP_HID, NSA_DV)
    hn = NSA_HEADS
    for layer in range(N_A_LAYERS, DEPTH):
        inputs[f'nsa_w_in_{layer}'] = jnp.concatenate([
            dense(D_MODEL, hn * NSA_DK), dense(D_MODEL, hn * NSA_DV), dense(D_MODEL, hn * NSA_DV),
            dense(D_MODEL, hn * NSA_DV), dense(D_MODEL, N_BRANCH * hn)], axis=1)
        inputs[f'nsa_w_out_{layer}'] = dense(hn * NSA_DV, D_MODEL, BETA)
        gain, bias = norm_pair()
        inputs[f'ln_g_{layer}'] = gain
        inputs[f'ln_b_{layer}'] = bias
    return inputs


def reference(x, positions, ret_w_in_0, ret_w_out_0, ln_g_0, ln_b_0, ret_w_in_1, ret_w_out_1, ln_g_1, ln_b_1,
              nsa_w_kv, nsa_pe_k, nsa_pe_v, nsa_w_ck1, nsa_w_ck2, nsa_w_cv1, nsa_w_cv2,
              nsa_w_in_2, nsa_w_out_2, ln_g_2, ln_b_2, nsa_w_in_3, nsa_w_out_3, ln_g_3, ln_b_3):
    mixer_params = [(ret_w_in_0, ret_w_out_0), (ret_w_in_1, ret_w_out_1),
                    (nsa_w_in_2, nsa_w_out_2), (nsa_w_in_3, nsa_w_out_3)]
    norm_params = [(ln_g_0, ln_b_0), (ln_g_1, ln_b_1), (ln_g_2, ln_b_2), (ln_g_3, ln_b_3)]
    shared = None
    for layer in range(DEPTH):
        w_in, w_out = mixer_params[layer]
        if layer < N_A_LAYERS:
            y = retention_mixer(x, positions, w_in, w_out)
        else:
            if layer == N_A_LAYERS:
                shared = nsa_shared_kv(x, nsa_w_kv, nsa_pe_k, nsa_pe_v, nsa_w_ck1, nsa_w_ck2, nsa_w_cv1, nsa_w_cv2)
            y = nsa_mixer(x, shared, w_in, w_out)
        gain, bias = norm_params[layer]
        x = layer_norm(ALPHA * x + y, gain, bias)
    return x
```

```python
import functools

import jax
import jax.numpy as jnp
from jax import lax
from jax.experimental import pallas as pl
from jax.experimental.pallas import tpu as pltpu

D_MODEL = 2048
DEPTH = 4
N_A_LAYERS = DEPTH // 2
RET_HEADS = 8
RET_DK = D_MODEL // RET_HEADS
RET_DV = 2 * D_MODEL // RET_HEADS
RET_CHUNK = 128
RET_THETA_BASE = 10000.0
NSA_HEADS = 16
NSA_GROUPS = 4
NSA_HPG = NSA_HEADS // NSA_GROUPS
NSA_DV = D_MODEL // NSA_HEADS
NSA_DK = 3 * NSA_DV // 2
NSA_DKP = 256
CMP_LEN = 32
CMP_STRIDE = 16
CMP_HID = 2 * NSA_DV
SEL_LEN = 64
SEL_TOPK = 16
WINDOW = 512
N_BRANCH = 3
ALPHA = (2.0 * DEPTH) ** 0.25
NEG_INF = -1e30
FORCE_SCORE = 1e9
LN_EPS = 1e-5

LANES = 128
ATT_TQ = 128
SEL_TK = 256
VMEM_LIMIT = 56 * 1024 * 1024

F32 = jnp.float32
BF16 = jnp.bfloat16
NT_DIMS = (((1,), (1,)), ((), ()))
TN_DIMS = (((0,), (0,)), ((), ()))


def _params(n_axes):
    return pltpu.CompilerParams(dimension_semantics=("arbitrary",) * n_axes, vmem_limit_bytes=VMEM_LIMIT)


def _silu(x):
    return x * jax.nn.sigmoid(x)


def _proj_kernel(x_ref, w_ref, o_ref, wbf_ref):
    @pl.when(pl.program_id(1) == 0)
    def _():
        wbf_ref[...] = w_ref[...].astype(BF16)

    o_ref[...] = jnp.dot(x_ref[...], wbf_ref[...], preferred_element_type=F32).astype(o_ref.dtype)


def _proj(x, w, *, n_cols, col_block0, tm, tn, out_dtype):
    m, k = x.shape
    return pl.pallas_call(
        _proj_kernel,
        grid=(n_cols // tn, m // tm),
        in_specs=[pl.BlockSpec((tm, k), lambda j, i: (i, 0)),
                  pl.BlockSpec((k, tn), lambda j, i: (0, j + col_block0))],
        out_specs=pl.BlockSpec((tm, tn), lambda j, i: (i, j)),
        out_shape=jax.ShapeDtypeStruct((m, n_cols), out_dtype),
        scratch_shapes=[pltpu.VMEM((k, tn), BF16)],
        compiler_params=_params(2),
        name="proj",
    )(x, w)


def _out_ln_kernel(a_ref, w_ref, x_ref, g_ref, b_ref, o32_ref, o16_ref, acc_ref):
    kk = pl.program_id(1)

    @pl.when(kk == 0)
    def _():
        acc_ref[...] = jnp.zeros_like(acc_ref)

    acc_ref[...] += jnp.dot(a_ref[...], w_ref[...].astype(BF16), preferred_element_type=F32)

    @pl.when(kk == pl.num_programs(1) - 1)
    def _():
        u = ALPHA * x_ref[...] + acc_ref[...]
        mu = jnp.mean(u, axis=-1, keepdims=True)
        var = jnp.mean(jnp.square(u - mu), axis=-1, keepdims=True)
        y = (u - mu) * lax.rsqrt(var + LN_EPS) * g_ref[...] + b_ref[...]
        o32_ref[...] = y
        o16_ref[...] = y.astype(BF16)


def _out_ln(a, w, x, gain, bias, *, tm=512, tk=512):
    m, ka = a.shape
    d = w.shape[1]
    return pl.pallas_call(
        _out_ln_kernel,
        grid=(m // tm, ka // tk),
        in_specs=[pl.BlockSpec((tm, tk), lambda i, k: (i, k)),
                  pl.BlockSpec((tk, d), lambda i, k: (k, 0)),
                  pl.BlockSpec((tm, d), lambda i, k: (i, 0)),
                  pl.BlockSpec((1, d), lambda i, k: (0, 0)),
                  pl.BlockSpec((1, d), lambda i, k: (0, 0))],
        out_specs=[pl.BlockSpec((tm, d), lambda i, k: (i, 0)),
                   pl.BlockSpec((tm, d), lambda i, k: (i, 0))],
        out_shape=[jax.ShapeDtypeStruct((m, d), F32), jax.ShapeDtypeStruct((m, d), BF16)],
        scratch_shapes=[pltpu.VMEM((tm, d), F32)],
        compiler_params=_params(2),
        name="out_ln",
    )(a, w, x, gain.reshape(1, d), bias.reshape(1, d))


def _rope_kernel(pos_ref, freq_ref, cos_ref, sin_ref):
    ang = pos_ref[...].astype(F32) * freq_ref[...]
    cos_ref[...] = jnp.cos(ang)
    sin_ref[...] = jnp.sin(ang)


def _rope_tables(positions, tm=1024):
    t = positions.size
    half = RET_DK // 2
    inv_freq = 1.0 / (RET_THETA_BASE ** jnp.linspace(0.0, 1.0, half, dtype=F32))
    return pl.pallas_call(
        _rope_kernel,
        grid=(t // tm,),
        in_specs=[pl.BlockSpec((tm, 1), lambda i: (i, 0)), pl.BlockSpec((1, half), lambda i: (0, 0))],
        out_specs=[pl.BlockSpec((tm, half), lambda i: (i, 0))] * 2,
        out_shape=[jax.ShapeDtypeStruct((t, half), F32)] * 2,
        compiler_params=_params(1),
        name="rope_tables",
    )(positions.reshape(t, 1), inv_freq.reshape(1, half))


def _retention_kernel(q_ref, k_ref, v_ref, z_ref, cos_ref, sin_ref, dintra_ref, dq_ref, dk_ref, dch_ref,
                      o_ref, state_ref, *, n_sub):
    c = RET_CHUNK
    half = RET_DK // 2

    @pl.when(pl.program_id(2) == 0)
    def _():
        state_ref[...] = jnp.zeros_like(state_ref)

    def rotate(t, cosv, sinv):
        even, odd = t[:, :half], t[:, half:]
        return jnp.concatenate([even * cosv - odd * sinv, even * sinv + odd * cosv], axis=1)

    for j in range(n_sub):
        rows = pl.ds(j * c, c)
        cosv, sinv = cos_ref[rows, :], sin_ref[rows, :]
        qr = rotate(q_ref[rows, :].astype(F32), cosv, sinv)
        kr = rotate(k_ref[rows, :].astype(F32), cosv, sinv) * (RET_DK ** -0.5)
        qb, kb = qr.astype(BF16), kr.astype(BF16)
        kdb = (kr * dk_ref[0]).astype(BF16)
        v = v_ref[rows, :]
        state = state_ref[...]
        scores = lax.dot_general(qb, kb, NT_DIMS, preferred_element_type=F32) * dintra_ref[0]
        out = (jnp.dot(scores.astype(BF16), v, preferred_element_type=F32)
               + jnp.dot(qb, state.astype(BF16), preferred_element_type=F32) * dq_ref[0])
        state_ref[...] = state * dch_ref[0] + lax.dot_general(kdb, v, TN_DIMS, preferred_element_type=F32)
        mu = jnp.mean(out, axis=-1, keepdims=True)
        var = jnp.mean(jnp.square(out - mu), axis=-1, keepdims=True)
        normed = (out - mu) * lax.rsqrt(var + LN_EPS)
        o_ref[rows, :] = (normed * _silu(z_ref[rows, :].astype(F32))).astype(BF16)


def _retention_decays():
    h, c = RET_HEADS, RET_CHUNK
    log_gamma = jnp.log1p(-jnp.exp2(-5.0 - jnp.arange(h, dtype=F32)))
    idx = jnp.arange(c, dtype=F32)
    rel = idx[:, None] - idx[None, :]
    dintra = jnp.where(rel >= 0, jnp.exp(log_gamma[:, None, None] * jnp.maximum(rel, 0.0)), 0.0)
    dq = jnp.exp(log_gamma[:, None] * (idx + 1.0))
    dk = jnp.exp(log_gamma[:, None] * (c - 1.0 - idx))
    dch = jnp.exp(log_gamma * c)
    return (dintra,
            jnp.broadcast_to(dq[:, :, None], (h, c, RET_DV)),
            jnp.broadcast_to(dk[:, :, None], (h, c, RET_DK)),
            jnp.broadcast_to(dch[:, None, None], (h, 1, RET_DV)))


def _retention(qk, vz, cos, sin, bsz, seq, *, ts=512):
    h, dk, dv, c = RET_HEADS, RET_DK, RET_DV, RET_CHUNK
    ns = seq // ts
    dintra, dq, dkt, dch = _retention_decays()
    row = lambda b, hh, s: b * ns + s
    return pl.pallas_call(
        functools.partial(_retention_kernel, n_sub=ts // c),
        grid=(bsz, h, ns),
        in_specs=[pl.BlockSpec((ts, dk), lambda b, hh, s: (row(b, hh, s), hh)),
                  pl.BlockSpec((ts, dk), lambda b, hh, s: (row(b, hh, s), h + hh)),
                  pl.BlockSpec((ts, dv), lambda b, hh, s: (row(b, hh, s), hh)),
                  pl.BlockSpec((ts, dv), lambda b, hh, s: (row(b, hh, s), h + hh)),
                  pl.BlockSpec((ts, dk // 2), lambda b, hh, s: (row(b, hh, s), 0)),
                  pl.BlockSpec((ts, dk // 2), lambda b, hh, s: (row(b, hh, s), 0)),
                  pl.BlockSpec((1, c, c), lambda b, hh, s: (hh, 0, 0)),
                  pl.BlockSpec((1, c, dv), lambda b, hh, s: (hh, 0, 0)),
                  pl.BlockSpec((1, c, dk), lambda b, hh, s: (hh, 0, 0)),
                  pl.BlockSpec((1, 1, dv), lambda b, hh, s: (hh, 0, 0))],
        out_specs=pl.BlockSpec((ts, dv), lambda b, hh, s: (row(b, hh, s), hh)),
        out_shape=jax.ShapeDtypeStruct((bsz * seq, h * dv), BF16),
        scratch_shapes=[pltpu.VMEM((dk, dv), F32)],
        compiler_params=_params(3),
        name="retention",
    )(qk, qk, vz, vz, cos, sin, dintra, dq, dkt, dch)


def _retention_layer(x32, x16, cos, sin, w_in, w_out, gain, bias, bsz, seq):
    h, dk, dv = RET_HEADS, RET_DK, RET_DV
    d = x16.shape[1]
    w_qk = w_in[:, :2 * h * dk].reshape(d, 2 * h, dk // 2, 2).transpose(0, 1, 3, 2).reshape(d, 2 * h * dk)
    qk = _proj(x16, w_qk, n_cols=2 * h * dk, col_block0=0, tm=512, tn=1024, out_dtype=BF16)
    vz = _proj(x16, w_in, n_cols=2 * h * dv, col_block0=2 * h * dk // 1024, tm=512, tn=1024, out_dtype=BF16)
    mixed = _retention(qk, vz, cos, sin, bsz, seq)
    return _out_ln(mixed, w_out, x32, gain, bias)


def _compress_kernel(*refs):
    x_refs, (pe_ref, w1_ref, w2_ref, o_ref) = refs[:-4], refs[-4:]
    n_blk = o_ref.shape[2]
    half = CMP_LEN // 2
    acc_lo = jnp.zeros((n_blk, CMP_HID), F32)
    acc_hi = jnp.zeros((n_blk, CMP_HID), F32)
    for l in range(half):
        rows = jnp.concatenate([x_ref[pl.ds(l, n_blk, stride=CMP_STRIDE), :] for x_ref in x_refs], axis=1)
        acc_lo += jnp.dot((rows + pe_ref[l:l + 1, :]).astype(BF16), w1_ref[l], preferred_element_type=F32)
        acc_hi += jnp.dot((rows + pe_ref[half + l:half + l + 1, :]).astype(BF16), w1_ref[half + l],
                          preferred_element_type=F32)
    hid = acc_lo + pltpu.roll(acc_hi, shift=n_blk - 1, axis=0)
    out = jnp.dot(_silu(hid).astype(BF16), w2_ref[...], preferred_element_type=F32)
    keep = lax.broadcasted_iota(jnp.int32, out.shape, 0) < n_blk - 1
    o_ref[0, 0] = jnp.where(keep, out, 0.0).astype(BF16)


def _compress(kvc, pe, w1, w2, bsz, seq, *, width, col_block0):
    g = NSA_GROUPS
    n_blk = seq // CMP_STRIDE
    n_slabs = width // LANES
    slab = lambda j: pl.BlockSpec((seq, LANES), lambda b, gg: (b, (col_block0 + gg) * n_slabs + j))
    return pl.pallas_call(
        _compress_kernel,
        grid=(bsz, g),
        in_specs=[slab(j) for j in range(n_slabs)] + [
                  pl.BlockSpec(pe.shape, lambda b, gg: (0, 0)),
                  pl.BlockSpec(w1.shape, lambda b, gg: (0, 0, 0)),
                  pl.BlockSpec(w2.shape, lambda b, gg: (0, 0))],
        out_specs=pl.BlockSpec((1, 1, n_blk, w2.shape[1]), lambda b, gg: (b, gg, 0, 0)),
        out_shape=jax.ShapeDtypeStruct((bsz, g, n_blk, w2.shape[1]), BF16),
        compiler_params=_params(2),
        name="compress",
    )(*([kvc] * n_slabs), pe, w1, w2)


def _pad_last(a, width):
    return jnp.pad(a, [(0, 0)] * (a.ndim - 1) + [(0, width - a.shape[-1])])


def _nsa_shared_kv(x16, w_kv, pe_k, pe_v, w_ck1, w_ck2, w_cv1, w_cv2, bsz, seq):
    g, dk, dv, dkp = NSA_GROUPS, NSA_DK, NSA_DV, NSA_DKP
    d = x16.shape[1]
    sizes = [g * dk, g * dv] * N_BRANCH
    offs = [0]
    for s in sizes:
        offs.append(offs[-1] + s)
    parts = [w_kv[:, offs[i]:offs[i + 1]] for i in range(len(sizes))]
    padk = lambda w: _pad_last(w.reshape(d, g, dk), dkp).reshape(d, g * dkp)
    w_cmp = jnp.concatenate([padk(parts[0]), parts[1]], axis=1)
    w_rest = jnp.concatenate([padk(parts[2]), parts[3], padk(parts[4]), parts[5]], axis=1)
    wc = g * (dkp + dv)
    kvc = _proj(x16, w_cmp, n_cols=wc, col_block0=0, tm=512, tn=wc // 2, out_dtype=F32)
    kv = _proj(x16, w_rest, n_cols=2 * wc, col_block0=0, tm=512, tn=wc // 2, out_dtype=BF16)
    w1k = _pad_last(w_ck1.reshape(CMP_LEN, dk, CMP_HID).transpose(0, 2, 1), dkp).transpose(0, 2, 1).astype(BF16)
    w1v = w_cv1.reshape(CMP_LEN, dv, CMP_HID).astype(BF16)
    kc = _compress(kvc, _pad_last(pe_k, dkp), w1k, _pad_last(w_ck2, dkp).astype(BF16), bsz, seq,
                   width=dkp, col_block0=0)
    vc = _compress(kvc, pe_v, w1v, w_cv2.astype(BF16), bsz, seq, width=dv, col_block0=g * dkp // dv)
    return kc, vc, kv


def _stack_heads(q_ref):
    return jnp.concatenate([q_ref[:, r * NSA_DKP:(r + 1) * NSA_DKP] for r in range(NSA_HPG)], axis=0)


def _cmp_kernel(q_ref, kc_ref, vc_ref, ov_ref, o_ref, sel_ref):
    tq = q_ref.shape[0]
    r = NSA_HPG
    n_blk = kc_ref.shape[2]
    scale = NSA_DK ** -0.5
    q0 = pl.program_id(2) * tq
    q4 = _stack_heads(q_ref)
    s = lax.dot_general(q4, kc_ref[0, 0], NT_DIMS, preferred_element_type=F32) * scale
    tpos = q0 + lax.rem(lax.broadcasted_iota(jnp.int32, s.shape, 0), tq)
    blk_end = lax.broadcasted_iota(jnp.int32, s.shape, 1) * CMP_STRIDE + (CMP_LEN - 1)
    vis = blk_end <= tpos
    s = jnp.where(vis, s, NEG_INF)
    m = jnp.max(s, axis=-1, keepdims=True)
    e = jnp.where(vis, jnp.exp(s - m), 0.0)
    l = jnp.sum(e, axis=-1, keepdims=True)
    p = jnp.where(l > 0.0, e / jnp.where(l > 0.0, l, 1.0), 0.0)
    pb = p.astype(BF16)
    o = jnp.dot(pb, vc_ref[0, 0], preferred_element_type=F32)
    for h in range(r):
        o_ref[:, h * NSA_DV:(h + 1) * NSA_DV] = o[h * tq:(h + 1) * tq, :]

    p_sel = sum(jnp.dot(pb[h * tq:(h + 1) * tq, :], ov_ref[...], preferred_element_type=F32) for h in range(r))
    ps_t = p_sel.T
    blk = lax.broadcasted_iota(jnp.int32, ps_t.shape, 0)
    cur = (q0 + lax.broadcasted_iota(jnp.int32, ps_t.shape, 1)) // SEL_LEN
    forced = (blk == 0) | (blk == cur) | (blk == cur - 1)
    score = jnp.where(forced, FORCE_SCORE, jnp.where(blk <= cur, ps_t, -1.0))
    n_sel = n_blk * CMP_STRIDE // SEL_LEN
    score = score[:n_sel, :]
    blk = blk[:n_sel, :]
    rank = jnp.zeros(score.shape, F32)
    for i in range(n_sel):
        row = score[i:i + 1, :]
        ahead = (row > score) | ((row == score) & (blk > i))
        rank += jnp.where(ahead, 1.0, 0.0)
    chosen = jnp.where(rank < float(min(SEL_TOPK, n_sel)), 1.0, 0.0)
    chosen = jnp.concatenate([chosen, jnp.zeros((LANES - n_sel, tq), F32)], axis=0) if n_sel < LANES else chosen
    sel_ref[0, 0] = chosen.T.astype(BF16)


def _overlap_matrix(n_blk, n_sel):
    a, b = SEL_LEN // CMP_STRIDE, CMP_LEN // CMP_STRIDE
    span = a + b - 2
    n = jnp.arange(n_blk)[:, None]
    diff = n - a * jnp.arange(LANES)[None, :]
    ov = jnp.where((diff >= 0) & (diff <= span), jnp.minimum(jnp.minimum(diff, span - diff), min(a, b) - 1) + 1, 0)
    valid = (n < n_blk - 1) & (jnp.arange(LANES)[None, :] < n_sel)
    return jnp.where(valid, ov, 0).astype(BF16)


def _cmp_branch(proj, kc, vc, bsz, seq):
    g, r, dv, dkp, tq = NSA_GROUPS, NSA_HPG, NSA_DV, NSA_DKP, ATT_TQ
    nq = seq // tq
    n_blk = seq // CMP_STRIDE
    ov = _overlap_matrix(n_blk, seq // SEL_LEN)
    return pl.pallas_call(
        _cmp_kernel,
        grid=(bsz, g, nq),
        in_specs=[pl.BlockSpec((tq, r * dkp), lambda b, gg, i: (b * nq + i, gg)),
                  pl.BlockSpec((1, 1, n_blk, dkp), lambda b, gg, i: (b, gg, 0, 0)),
                  pl.BlockSpec((1, 1, n_blk, dv), lambda b, gg, i: (b, gg, 0, 0)),
                  pl.BlockSpec(ov.shape, lambda b, gg, i: (0, 0))],
        out_specs=[pl.BlockSpec((tq, r * dv), lambda b, gg, i: (b * nq + i, gg)),
                   pl.BlockSpec((1, 1, tq, LANES), lambda b, gg, i: (b, gg, i, 0))],
        out_shape=[jax.ShapeDtypeStruct((bsz * seq, g * r * dv), F32),
                   jax.ShapeDtypeStruct((bsz, g, seq, LANES), BF16)],
        compiler_params=_params(3),
        name="cmp_branch",
    )(proj, kc, vc, ov)


def _sel_kernel(q_ref, sel_ref, k_ref, v_ref, o_ref, m_ref, l_ref, acc_ref):
    tq, tk, r = q_ref.shape[0], SEL_TK, NSA_HPG
    scale = NSA_DK ** -0.5
    i = pl.program_id(2)
    q4 = _stack_heads(q_ref)
    selb = sel_ref[0, 0]
    m_ref[...] = jnp.full(m_ref.shape, NEG_INF, F32)
    l_ref[...] = jnp.zeros_like(l_ref)
    acc_ref[...] = jnp.zeros_like(acc_ref)
    qpos = i * tq + lax.broadcasted_iota(jnp.int32, (tq, tk), 0)
    col = lax.broadcasted_iota(jnp.int32, (tq, tk), 1)
    e_blk = lax.broadcasted_iota(jnp.int32, (LANES, tk), 0)
    e_col = lax.broadcasted_iota(jnp.int32, (LANES, tk), 1) // SEL_LEN

    def body(kt, carry):
        k0 = pl.multiple_of(kt * tk, tk)
        k = k_ref[pl.ds(k0, tk), :]
        v = v_ref[pl.ds(k0, tk), :]
        s = lax.dot_general(q4, k, NT_DIMS, preferred_element_type=F32) * scale
        expand = jnp.where(e_blk == kt * (tk // SEL_LEN) + e_col, 1.0, 0.0).astype(BF16)
        picked = jnp.dot(selb, expand, preferred_element_type=F32)
        ok = (picked > 0.5) & (k0 + col <= qpos)
        bias = jnp.where(ok, 0.0, NEG_INF)
        s = s + jnp.concatenate([bias] * r, axis=0)
        m_prev = m_ref[...]
        m_new = jnp.maximum(m_prev, jnp.max(s, axis=-1, keepdims=True))
        alpha = jnp.exp(m_prev - m_new)
        p = jnp.exp(s - m_new)
        l_ref[...] = alpha * l_ref[...] + jnp.sum(p, axis=-1, keepdims=True)
        acc_ref[...] = alpha * acc_ref[...] + jnp.dot(p.astype(BF16), v, preferred_element_type=F32)
        m_ref[...] = m_new
        return carry

    lax.fori_loop(0, (i * tq) // tk + 1, body, 0)
    o = acc_ref[...] / l_ref[...]
    for h in range(r):
        o_ref[:, h * NSA_DV:(h + 1) * NSA_DV] = o[h * tq:(h + 1) * tq, :]


def _sel_branch(proj, sel, kv, bsz, seq):
    g, r, dv, dkp, tq = NSA_GROUPS, NSA_HPG, NSA_DV, NSA_DKP, ATT_TQ
    nq = seq // tq
    kblk0 = 0
    vblk0 = g * dkp // dv
    return pl.pallas_call(
        _sel_kernel,
        grid=(bsz, g, nq),
        in_specs=[pl.BlockSpec((tq, r * dkp), lambda b, gg, i: (b * nq + i, gg)),
                  pl.BlockSpec((1, 1, tq, LANES), lambda b, gg, i: (b, gg, i, 0)),
                  pl.BlockSpec((seq, dkp), lambda b, gg, i: (b, kblk0 + gg)),
                  pl.BlockSpec((seq, dv), lambda b, gg, i: (b, vblk0 + gg))],
        out_specs=pl.BlockSpec((tq, r * dv), lambda b, gg, i: (b * nq + i, gg)),
        out_shape=jax.ShapeDtypeStruct((bsz * seq, g * r * dv), F32),
        scratch_shapes=[pltpu.VMEM((r * tq, 1), F32), pltpu.VMEM((r * tq, 1), F32), pltpu.VMEM((r * tq, dv), F32)],
        compiler_params=_params(3),
        name="sel_branch",
    )(proj, sel, kv, kv)


def _win_kernel(q_ref, k_ref, v_ref, o_ref):
    tq, r = q_ref.shape[0], NSA_HPG
    span = WINDOW + tq
    scale = NSA_DK ** -0.5
    i = pl.program_id(2)
    q4 = _stack_heads(q_ref)
    k0 = pl.multiple_of(jnp.maximum(i * tq - WINDOW, 0), tq)
    k = k_ref[pl.ds(k0, span), :]
    v = v_ref[pl.ds(k0, span), :]
    s = lax.dot_general(q4, k, NT_DIMS, preferred_element_type=F32) * scale
    qpos = i * tq + lax.rem(lax.broadcasted_iota(jnp.int32, s.shape, 0), tq)
    dist = qpos - (k0 + lax.broadcasted_iota(jnp.int32, s.shape, 1))
    ok = (dist >= 0) & (dist < WINDOW)
    s = jnp.where(ok, s, NEG_INF)
    e = jnp.exp(s - jnp.max(s, axis=-1, keepdims=True))
    p = e / jnp.sum(e, axis=-1, keepdims=True)
    o = jnp.dot(p.astype(BF16), v, preferred_element_type=F32)
    for h in range(r):
        o_ref[:, h * NSA_DV:(h + 1) * NSA_DV] = o[h * tq:(h + 1) * tq, :]


def _win_branch(proj, kv, bsz, seq):
    g, r, dv, dkp, tq = NSA_GROUPS, NSA_HPG, NSA_DV, NSA_DKP, ATT_TQ
    nq = seq // tq
    kblk0 = g * (dkp + dv) // dkp
    vblk0 = (g * (dkp + dv) + g * dkp) // dv
    return pl.pallas_call(
        _win_kernel,
        grid=(bsz, g, nq),
        in_specs=[pl.BlockSpec((tq, r * dkp), lambda b, gg, i: (b * nq + i, gg)),
                  pl.BlockSpec((seq, dkp), lambda b, gg, i: (b, kblk0 + gg)),
                  pl.BlockSpec((seq, dv), lambda b, gg, i: (b, vblk0 + gg))],
        out_specs=pl.BlockSpec((tq, r * dv), lambda b, gg, i: (b * nq + i, gg)),
        out_shape=jax.ShapeDtypeStruct((bsz * seq, g * r * dv), F32),
        compiler_params=_params(3),
        name="win_branch",
    )(proj, kv, kv)


def _combine_kernel(oc_ref, os_ref, ow_ref, zc_ref, zs_ref, zw_ref, gate_ref, o_ref):
    gates = jax.nn.sigmoid(gate_ref[...])
    branches = ((oc_ref, zc_ref), (os_ref, zs_ref), (ow_ref, zw_ref))
    for h in range(NSA_HEADS):
        cols = slice(h * NSA_DV, (h + 1) * NSA_DV)
        total = None
        for br, (o_br, z_br) in enumerate(branches):
            gcol = gates[:, br * NSA_HEADS + h:br * NSA_HEADS + h + 1]
            term = gcol * o_br[:, cols] * _silu(z_br[:, cols].astype(F32))
            total = term if total is None else total + term
        o_ref[:, cols] = total.astype(BF16)


def _combine(o_cmp, o_sel, o_win, proj, gate, *, tm=256):
    t, hd = o_cmp.shape
    zblk0 = NSA_HEADS * NSA_DKP // hd
    o_spec = pl.BlockSpec((tm, hd), lambda i: (i, 0))
    z_spec = lambda br: pl.BlockSpec((tm, hd), lambda i: (i, zblk0 + br))
    return pl.pallas_call(
        _combine_kernel,
        grid=(t // tm,),
        in_specs=[o_spec, o_spec, o_spec, z_spec(0), z_spec(1), z_spec(2),
                  pl.BlockSpec((tm, LANES), lambda i: (i, 0))],
        out_specs=pl.BlockSpec((tm, hd), lambda i: (i, 0)),
        out_shape=jax.ShapeDtypeStruct((t, hd), BF16),
        compiler_params=_params(1),
        name="combine",
    )(o_cmp, o_sel, o_win, proj, proj, proj, gate)


def _nsa_layer(x32, x16, shared, w_in, w_out, gain, bias, bsz, seq):
    kc, vc, kv = shared
    h, dk, dv, dkp = NSA_HEADS, NSA_DK, NSA_DV, NSA_DKP
    d = x16.shape[1]
    nq = h * dk
    nz = N_BRANCH * h * dv
    w_main = jnp.concatenate([_pad_last(w_in[:, :nq].reshape(d, h, dk), dkp).reshape(d, h * dkp),
                              w_in[:, nq:nq + nz]], axis=1)
    w_gate = _pad_last(w_in[:, nq + nz:], LANES)
    proj = _proj(x16, w_main, n_cols=h * dkp + nz, col_block0=0, tm=512, tn=1024, out_dtype=BF16)
    gate = _proj(x16, w_gate, n_cols=LANES, col_block0=0, tm=1024, tn=LANES, out_dtype=F32)
    o_cmp, sel = _cmp_branch(proj, kc, vc, bsz, seq)
    o_sel = _sel_branch(proj, sel, kv, bsz, seq)
    o_win = _win_branch(proj, kv, bsz, seq)
    mixed = _combine(o_cmp, o_sel, o_win, proj, gate)
    return _out_ln(mixed, w_out, x32, gain, bias)


def kernel(x, positions, ret_w_in_0, ret_w_out_0, ln_g_0, ln_b_0, ret_w_in_1, ret_w_out_1, ln_g_1, ln_b_1,
           nsa_w_kv, nsa_pe_k, nsa_pe_v, nsa_w_ck1, nsa_w_ck2, nsa_w_cv1, nsa_w_cv2,
           nsa_w_in_2, nsa_w_out_2, ln_g_2, ln_b_2, nsa_w_in_3, nsa_w_out_3, ln_g_3, ln_b_3):
    bsz, seq, d = x.shape
    x32 = x.reshape(bsz * seq, d)
    x16 = x32.astype(BF16)
    cos, sin = _rope_tables(positions)
    x32, x16 = _retention_layer(x32, x16, cos, sin, ret_w_in_0, ret_w_out_0, ln_g_0, ln_b_0, bsz, seq)
    x32, x16 = _retention_layer(x32, x16, cos, sin, ret_w_in_1, ret_w_out_1, ln_g_1, ln_b_1, bsz, seq)
    shared = _nsa_shared_kv(x16, nsa_w_kv, nsa_pe_k, nsa_pe_v, nsa_w_ck1, nsa_w_ck2, nsa_w_cv1, nsa_w_cv2, bsz, seq)
    x32, x16 = _nsa_layer(x32, x16, shared, nsa_w_in_2, nsa_w_out_2, ln_g_2, ln_b_2, bsz, seq)
    x32, x16 = _nsa_layer(x32, x16, shared, nsa_w_in_3, nsa_w_out_3, ln_g_3, ln_b_3, bsz, seq)
    return x32.reshape(bsz, seq, d)
```

```python
import functools
import math

import numpy as np
import jax
import jax.numpy as jnp
from jax import lax
from jax.experimental import pallas as pl
from jax.experimental.pallas import tpu as pltpu

D_MODEL = 2048
DEPTH = 4
N_A_LAYERS = DEPTH // 2
RET_HEADS = 8
RET_DK = D_MODEL // RET_HEADS
RET_DV = 2 * D_MODEL // RET_HEADS
RET_CHUNK = 128
RET_THETA_BASE = 10000.0
NSA_HEADS = 16
NSA_GROUPS = 4
NSA_HPG = NSA_HEADS // NSA_GROUPS
NSA_DV = D_MODEL // NSA_HEADS
NSA_DK = 3 * NSA_DV // 2
NSA_DKP = 256
CMP_LEN = 32
CMP_STRIDE = 16
CMP_HID = 2 * NSA_DV
SEL_LEN = 64
SEL_TOPK = 16
WINDOW = 512
N_BRANCH = 3
ALPHA = (2.0 * DEPTH) ** 0.25
NEG_INF = -1e30
FORCE_SCORE = 1e9
LN_EPS = 1e-5

LANES = 128
SUBLANES = 8
ATT_TQ = 128
SEL_TK = 512
VMEM_LIMIT = 56 * 1024 * 1024

F32 = jnp.float32
BF16 = jnp.bfloat16
NT_DIMS = (((1,), (1,)), ((), ()))
TN_DIMS = (((0,), (0,)), ((), ()))


def _params(n_axes):
    return pltpu.CompilerParams(dimension_semantics=("arbitrary",) * n_axes, vmem_limit_bytes=VMEM_LIMIT)


def _silu(x):
    return x * jax.nn.sigmoid(x)


def _proj_kernel(x_ref, w_ref, *rest):
    perm_ref = rest[0] if len(rest) == 3 else None
    o_ref, wbf_ref = rest[-2:]

    @pl.when(pl.program_id(1) == 0)
    def _():
        if perm_ref is None:
            wbf_ref[...] = w_ref[...].astype(BF16)
        else:
            blk = perm_ref.shape[0]
            for c in range(w_ref.shape[1] // blk):
                cols = slice(c * blk, (c + 1) * blk)
                wbf_ref[:, cols] = jnp.dot(w_ref[:, cols].astype(BF16), perm_ref[...],
                                           preferred_element_type=F32).astype(BF16)

    o_ref[...] = jnp.dot(x_ref[...], wbf_ref[...], preferred_element_type=F32).astype(o_ref.dtype)


def _proj(x, w, *, n_cols, col_block0, tm, tn, out_dtype, perm=None):
    m, k = x.shape
    in_specs = [pl.BlockSpec((tm, k), lambda j, i: (i, 0)),
                pl.BlockSpec((k, tn), lambda j, i: (0, j + col_block0))]
    args = [x, w]
    if perm is not None:
        in_specs.append(pl.BlockSpec(perm.shape, lambda j, i: (0, 0)))
        args.append(perm)
    return pl.pallas_call(
        _proj_kernel,
        grid=(n_cols // tn, m // tm),
        in_specs=in_specs,
        out_specs=pl.BlockSpec((tm, tn), lambda j, i: (i, j)),
        out_shape=jax.ShapeDtypeStruct((m, n_cols), out_dtype),
        scratch_shapes=[pltpu.VMEM((k, tn), BF16)],
        compiler_params=_params(2),
        name="proj",
    )(*args)


def _out_ln_kernel(a_ref, w_ref, x_ref, g_ref, b_ref, o32_ref, o16_ref, acc_ref):
    kk = pl.program_id(1)

    @pl.when(kk == 0)
    def _():
        acc_ref[...] = jnp.zeros_like(acc_ref)

    acc_ref[...] += jnp.dot(a_ref[...], w_ref[...], preferred_element_type=F32)

    @pl.when(kk == pl.num_programs(1) - 1)
    def _():
        u = ALPHA * x_ref[...] + acc_ref[...]
        mu = jnp.mean(u, axis=-1, keepdims=True)
        var = jnp.mean(jnp.square(u - mu), axis=-1, keepdims=True)
        y = (u - mu) * lax.rsqrt(var + LN_EPS) * g_ref[...] + b_ref[...]
        o32_ref[...] = y
        o16_ref[...] = y.astype(BF16)


def _out_ln(a, w, x, gain, bias, *, tm=512, tk=1024):
    m, ka = a.shape
    d = w.shape[1]
    return pl.pallas_call(
        _out_ln_kernel,
        grid=(m // tm, ka // tk),
        in_specs=[pl.BlockSpec((tm, tk), lambda i, k: (i, k)),
                  pl.BlockSpec((tk, d), lambda i, k: (k, 0)),
                  pl.BlockSpec((tm, d), lambda i, k: (i, 0)),
                  pl.BlockSpec((1, d), lambda i, k: (0, 0)),
                  pl.BlockSpec((1, d), lambda i, k: (0, 0))],
        out_specs=[pl.BlockSpec((tm, d), lambda i, k: (i, 0)),
                   pl.BlockSpec((tm, d), lambda i, k: (i, 0))],
        out_shape=[jax.ShapeDtypeStruct((m, d), F32), jax.ShapeDtypeStruct((m, d), BF16)],
        scratch_shapes=[pltpu.VMEM((tm, d), F32)],
        compiler_params=_params(2),
        name="out_ln",
    )(a, w, x, gain.reshape(1, d), bias.reshape(1, d))


def _rope_kernel(pos_ref, freq_ref, cos_ref, sin_ref):
    ang = pos_ref[...].astype(F32) * freq_ref[...]
    cos_ref[...] = jnp.cos(ang)
    sin_ref[...] = jnp.sin(ang)


def _rope_tables(positions, tm=1024):
    t = positions.size
    half = RET_DK // 2
    inv_freq = 1.0 / (RET_THETA_BASE ** jnp.linspace(0.0, 1.0, half, dtype=F32))
    return pl.pallas_call(
        _rope_kernel,
        grid=(t // tm,),
        in_specs=[pl.BlockSpec((tm, 1), lambda i: (i, 0)), pl.BlockSpec((1, half), lambda i: (0, 0))],
        out_specs=[pl.BlockSpec((tm, half), lambda i: (i, 0))] * 2,
        out_shape=[jax.ShapeDtypeStruct((t, half), F32)] * 2,
        compiler_params=_params(1),
        name="rope_tables",
    )(positions.reshape(t, 1), inv_freq.reshape(1, half))


def _retention_kernel(q_ref, k_ref, v_ref, z_ref, cos_ref, sin_ref, dintra_ref, dq_ref, dk_ref, dch_ref,
                      o_ref, state_ref, *, n_sub):
    c = RET_CHUNK
    half = RET_DK // 2

    @pl.when(pl.program_id(2) == 0)
    def _():
        state_ref[...] = jnp.zeros_like(state_ref)

    def rotate(t, cosv, sinv):
        even, odd = t[:, :half], t[:, half:]
        return jnp.concatenate([even * cosv - odd * sinv, even * sinv + odd * cosv], axis=1)

    for j in range(n_sub):
        rows = pl.ds(j * c, c)
        cosv, sinv = cos_ref[rows, :], sin_ref[rows, :]
        qr = rotate(q_ref[rows, :].astype(F32), cosv, sinv)
        kr = rotate(k_ref[rows, :].astype(F32), cosv, sinv) * (RET_DK ** -0.5)
        qb, kb = qr.astype(BF16), kr.astype(BF16)
        kdb = (kr * dk_ref[0]).astype(BF16)
        v = v_ref[rows, :]
        state = state_ref[...]
        scores = lax.dot_general(qb, kb, NT_DIMS, preferred_element_type=F32) * dintra_ref[0]
        out = (jnp.dot(scores.astype(BF16), v, preferred_element_type=F32)
               + jnp.dot(qb, state.astype(BF16), preferred_element_type=F32) * dq_ref[0])
        state_ref[...] = state * dch_ref[0] + lax.dot_general(kdb, v, TN_DIMS, preferred_element_type=F32)
        mu = jnp.mean(out, axis=-1, keepdims=True)
        var = jnp.mean(jnp.square(out - mu), axis=-1, keepdims=True)
        normed = (out - mu) * lax.rsqrt(var + LN_EPS)
        o_ref[rows, :] = (normed * _silu(z_ref[rows, :].astype(F32))).astype(BF16)


def _retention_decays():
    h, c = RET_HEADS, RET_CHUNK
    log_gamma = jnp.log1p(-jnp.exp2(-5.0 - jnp.arange(h, dtype=F32)))
    idx = jnp.arange(c, dtype=F32)
    rel = idx[:, None] - idx[None, :]
    dintra = jnp.where(rel >= 0, jnp.exp(log_gamma[:, None, None] * jnp.maximum(rel, 0.0)), 0.0)
    dq = jnp.exp(log_gamma[:, None] * (idx + 1.0))
    dk = jnp.exp(log_gamma[:, None] * (c - 1.0 - idx))
    dch = jnp.exp(log_gamma * c)
    return (dintra,
            jnp.broadcast_to(dq[:, :, None], (h, c, RET_DV)),
            jnp.broadcast_to(dk[:, :, None], (h, c, RET_DK)),
            jnp.broadcast_to(dch[:, None, None], (h, 1, RET_DV)))


def _retention(qk, vz, cos, sin, bsz, seq, *, ts=512):
    h, dk, dv, c = RET_HEADS, RET_DK, RET_DV, RET_CHUNK
    ns = seq // ts
    dintra, dq, dkt, dch = _retention_decays()
    row = lambda b, hh, s: b * ns + s
    return pl.pallas_call(
        functools.partial(_retention_kernel, n_sub=ts // c),
        grid=(bsz, h, ns),
        in_specs=[pl.BlockSpec((ts, dk), lambda b, hh, s: (row(b, hh, s), hh)),
                  pl.BlockSpec((ts, dk), lambda b, hh, s: (row(b, hh, s), h + hh)),
                  pl.BlockSpec((ts, dv), lambda b, hh, s: (row(b, hh, s), hh)),
                  pl.BlockSpec((ts, dv), lambda b, hh, s: (row(b, hh, s), h + hh)),
                  pl.BlockSpec((ts, dk // 2), lambda b, hh, s: (row(b, hh, s), 0)),
                  pl.BlockSpec((ts, dk // 2), lambda b, hh, s: (row(b, hh, s), 0)),
                  pl.BlockSpec((1, c, c), lambda b, hh, s: (hh, 0, 0)),
                  pl.BlockSpec((1, c, dv), lambda b, hh, s: (hh, 0, 0)),
                  pl.BlockSpec((1, c, dk), lambda b, hh, s: (hh, 0, 0)),
                  pl.BlockSpec((1, 1, dv), lambda b, hh, s: (hh, 0, 0))],
        out_specs=pl.BlockSpec((ts, dv), lambda b, hh, s: (row(b, hh, s), hh)),
        out_shape=jax.ShapeDtypeStruct((bsz * seq, h * dv), BF16),
        scratch_shapes=[pltpu.VMEM((dk, dv), F32)],
        compiler_params=_params(3),
        name="retention",
    )(qk, qk, vz, vz, cos, sin, dintra, dq, dkt, dch)


def _deinterleave_matrix(width):
    p = np.zeros((width, width), np.float32)
    j = np.arange(width // 2)
    p[2 * j, j] = 1.0
    p[2 * j + 1, width // 2 + j] = 1.0
    return jnp.asarray(p, BF16)


def _retention_layer(x32, x16, cos, sin, w_in, w_out, gain, bias, bsz, seq):
    h, dk, dv = RET_HEADS, RET_DK, RET_DV
    qk = _proj(x16, w_in, n_cols=2 * h * dk, col_block0=0, tm=512, tn=1024, out_dtype=BF16,
               perm=_deinterleave_matrix(dk))
    vz = _proj(x16, w_in, n_cols=2 * h * dv, col_block0=2 * h * dk // 1024, tm=512, tn=1024, out_dtype=BF16)
    mixed = _retention(qk, vz, cos, sin, bsz, seq)
    return _out_ln(mixed, w_out.astype(BF16), x32, gain, bias)


def _compress_kernel(*refs):
    x_refs, (pe_ref, w1_ref, w2_ref, o_ref) = refs[:-4], refs[-4:]
    n_blk = o_ref.shape[2]
    half = CMP_LEN // 2
    acc_lo = jnp.zeros((n_blk, CMP_HID), F32)
    acc_hi = jnp.zeros((n_blk, CMP_HID), F32)
    for l in range(half):
        rows = jnp.concatenate([x_ref[pl.ds(l, n_blk, stride=CMP_STRIDE), :] for x_ref in x_refs], axis=1)
        acc_lo += jnp.dot((rows + pe_ref[l:l + 1, :]).astype(BF16), w1_ref[l], preferred_element_type=F32)
        acc_hi += jnp.dot((rows + pe_ref[half + l:half + l + 1, :]).astype(BF16), w1_ref[half + l],
                          preferred_element_type=F32)
    hid = acc_lo + pltpu.roll(acc_hi, shift=n_blk - 1, axis=0)
    out = jnp.dot(_silu(hid).astype(BF16), w2_ref[...], preferred_element_type=F32)
    keep = lax.broadcasted_iota(jnp.int32, out.shape, 0) < n_blk - 1
    o_ref[0, 0] = jnp.where(keep, out, 0.0).astype(BF16)


def _compress(kvc, pe, w1, w2, bsz, seq, *, width, col_block0):
    g = NSA_GROUPS
    n_blk = seq // CMP_STRIDE
    n_slabs = width // LANES
    slab = lambda j: pl.BlockSpec((seq, LANES), lambda b, gg: (b, (col_block0 + gg) * n_slabs + j))
    return pl.pallas_call(
        _compress_kernel,
        grid=(bsz, g),
        in_specs=[slab(j) for j in range(n_slabs)] + [
                  pl.BlockSpec(pe.shape, lambda b, gg: (0, 0)),
                  pl.BlockSpec(w1.shape, lambda b, gg: (0, 0, 0)),
                  pl.BlockSpec(w2.shape, lambda b, gg: (0, 0))],
        out_specs=pl.BlockSpec((1, 1, n_blk, w2.shape[1]), lambda b, gg: (b, gg, 0, 0)),
        out_shape=jax.ShapeDtypeStruct((bsz, g, n_blk, w2.shape[1]), BF16),
        compiler_params=_params(2),
        name="compress",
    )(*([kvc] * n_slabs), pe, w1, w2)


def _pad_last(a, width):
    return jnp.pad(a, [(0, 0)] * (a.ndim - 1) + [(0, width - a.shape[-1])])


def _nsa_shared_kv(x16, w_kv, pe_k, pe_v, w_ck1, w_ck2, w_cv1, w_cv2, bsz, seq):
    g, dk, dv, dkp = NSA_GROUPS, NSA_DK, NSA_DV, NSA_DKP
    d = x16.shape[1]
    sizes = [g * dk, g * dv] * N_BRANCH
    offs = [0]
    for s in sizes:
        offs.append(offs[-1] + s)
    parts = [w_kv[:, offs[i]:offs[i + 1]] for i in range(len(sizes))]
    padk = lambda w: _pad_last(w.reshape(d, g, dk), dkp).reshape(d, g * dkp)
    w_cmp = jnp.concatenate([padk(parts[0]), parts[1]], axis=1)
    w_rest = jnp.concatenate([padk(parts[2]), parts[3], padk(parts[4]), parts[5]], axis=1)
    wc = g * (dkp + dv)
    kvc = _proj(x16, w_cmp, n_cols=wc, col_block0=0, tm=512, tn=wc // 2, out_dtype=F32)
    kv = _proj(x16, w_rest, n_cols=2 * wc, col_block0=0, tm=512, tn=wc // 2, out_dtype=BF16)
    w1k = _pad_last(w_ck1.reshape(CMP_LEN, dk, CMP_HID).transpose(0, 2, 1), dkp).transpose(0, 2, 1).astype(BF16)
    w1v = w_cv1.reshape(CMP_LEN, dv, CMP_HID).astype(BF16)
    kc = _compress(kvc, _pad_last(pe_k, dkp), w1k, _pad_last(w_ck2, dkp).astype(BF16), bsz, seq,
                   width=dkp, col_block0=0)
    vc = _compress(kvc, pe_v, w1v, w_cv2.astype(BF16), bsz, seq, width=dv, col_block0=g * dkp // dv)
    return kc, vc, kv


def _nsa_attn_kernel(q_ref, zc_ref, zs_ref, zw_ref, gate_ref, kc_ref, vc_ref, ovt_ref,
                     ks_ref, vs_ref, kw_ref, vw_ref, o_ref,
                     bias_ref, sa_ref, sb_ref, m_ref, l_ref, acc_ref, mix_ref, zsil_ref):
    tq, r, dv, tk = q_ref.shape[0], NSA_HPG, NSA_DV, SEL_TK
    lanes = r * tq
    n_blk = kc_ref.shape[2]
    n_sel = n_blk * CMP_STRIDE // SEL_LEN
    c_exp = (NSA_DK ** -0.5) * math.log2(math.e)
    q0 = pl.program_id(2) * tq
    q4 = jnp.concatenate([q_ref[:, h * NSA_DKP:(h + 1) * NSA_DKP] for h in range(r)], axis=0)

    def qpos(shape):
        return q0 + lax.rem(lax.broadcasted_iota(jnp.int32, shape, 1), tq)

    def softmax_terms(s):
        e = jnp.exp2((s - jnp.max(s, axis=0, keepdims=True)) * c_exp)
        return e, jnp.sum(e, axis=0, keepdims=True)

    def all_heads(tile):
        return jnp.concatenate([tile] * r, axis=1)

    gates_t = jax.nn.sigmoid(gate_ref[...]).T

    def gated(o_t, inv_l, silu_z, br, h):
        cols = slice(h * tq, (h + 1) * tq)
        row_scale = gates_t[br * r + h:br * r + h + 1, :] * inv_l[:, cols]
        return (o_t[:, cols] * row_scale).T * silu_z

    s = lax.dot_general(kc_ref[0, 0], q4, NT_DIMS, preferred_element_type=F32)
    blk_end = lax.broadcasted_iota(jnp.int32, (n_blk, tq), 0) * CMP_STRIDE + (CMP_LEN - 1)
    s = s + all_heads(jnp.where(blk_end <= qpos((n_blk, tq)), 0.0, NEG_INF))
    e, l = softmax_terms(s)
    inv_c = jnp.where(qpos((1, lanes)) >= CMP_LEN - 1, 1.0 / l, 0.0)
    eb = e.astype(BF16)
    o_cmp = lax.dot_general(vc_ref[0, 0], eb, TN_DIMS, preferred_element_type=F32)

    imp = sum(jnp.dot(ovt_ref[...], eb[:, h * tq:(h + 1) * tq], preferred_element_type=F32)
              * inv_c[:, h * tq:(h + 1) * tq] for h in range(r))
    imp = imp[:n_sel, :]
    blk = lax.broadcasted_iota(jnp.int32, imp.shape, 0)
    cur = (q0 + lax.broadcasted_iota(jnp.int32, imp.shape, 1)) // SEL_LEN
    forced = (blk == 0) | (blk == cur) | (blk == cur - 1)
    score = jnp.where(forced, FORCE_SCORE, jnp.where(blk <= cur, imp, -1.0))
    groups = [score[g0:g0 + SUBLANES, :] for g0 in range(0, n_sel, SUBLANES)]
    ranks = [jnp.zeros((SUBLANES, tq), F32) for _ in groups]
    for i in range(n_sel):
        row = score[i:i + 1, :]
        for gi, grp in enumerate(groups):
            g0 = gi * SUBLANES
            if g0 > i:
                ahead = row >= grp
            elif g0 + SUBLANES - 1 < i:
                ahead = row > grp
            else:
                after = lax.broadcasted_iota(jnp.int32, grp.shape, 0) + g0 > i
                ahead = (row > grp) | (after & (row == grp))
            ranks[gi] = ranks[gi] + jnp.where(ahead, 1.0, 0.0)
    rank = jnp.concatenate(ranks, axis=0)
    bias_ref[...] = jnp.where(rank < float(min(SEL_TOPK, n_sel)), 0.0, NEG_INF)

    span = WINDOW + tq
    k0 = pl.multiple_of(jnp.maximum(q0 - WINDOW, 0), tq)
    s = lax.dot_general(kw_ref[pl.ds(k0, span), :], q4, NT_DIMS, preferred_element_type=F32)
    dist = qpos((span, tq)) - (k0 + lax.broadcasted_iota(jnp.int32, (span, tq), 0))
    s = s + all_heads(jnp.where((dist >= 0) & (dist < WINDOW), 0.0, NEG_INF))
    e, l = softmax_terms(s)
    inv_w = 1.0 / l
    o_win = lax.dot_general(vw_ref[pl.ds(k0, span), :], e.astype(BF16), TN_DIMS, preferred_element_type=F32)

    for h in range(r):
        cols = slice(h * dv, (h + 1) * dv)
        mix_ref[:, cols] = (gated(o_cmp, inv_c, _silu(zc_ref[:, cols].astype(F32)), 0, h)
                            + gated(o_win, inv_w, _silu(zw_ref[:, cols].astype(F32)), 2, h))
    zsil_ref[...] = _silu(zs_ref[...].astype(F32))

    m_ref[...] = jnp.full(m_ref.shape, NEG_INF, F32)
    l_ref[...] = jnp.zeros_like(l_ref)
    acc_ref[...] = jnp.zeros_like(acc_ref)
    blocks_per_tile = tk // SEL_LEN
    n_tiles = q0 // tk + 1
    causal_gap = (q0 + lax.broadcasted_iota(jnp.int32, (tk, tq), 1)) - lax.broadcasted_iota(jnp.int32, (tk, tq), 0)

    def scores(tile):
        kbase = pl.multiple_of(tile * tk, tk)
        return lax.dot_general(ks_ref[pl.ds(kbase, tk), :], q4, NT_DIMS, preferred_element_type=F32)

    def attend(s_ref, tile, causal):
        kbase = pl.multiple_of(tile * tk, tk)
        rows = bias_ref[pl.ds(pl.multiple_of(tile * blocks_per_tile, blocks_per_tile), blocks_per_tile), :]
        bias = jnp.concatenate([jnp.broadcast_to(rows[j:j + 1, :], (SEL_LEN, tq)) for j in range(blocks_per_tile)],
                               axis=0)
        if causal:
            bias = jnp.where(causal_gap >= kbase, bias, NEG_INF)
        s = s_ref[...] + all_heads(bias)
        m_prev = m_ref[...]
        m_new = jnp.maximum(m_prev, jnp.max(s, axis=0, keepdims=True))
        alpha = jnp.exp2((m_prev - m_new) * c_exp)
        pt = jnp.exp2((s - m_new) * c_exp)
        l_ref[...] = alpha * l_ref[...] + jnp.sum(pt, axis=0, keepdims=True)
        acc_ref[...] = alpha * acc_ref[...] + lax.dot_general(vs_ref[pl.ds(kbase, tk), :], pt.astype(BF16), TN_DIMS,
                                                             preferred_element_type=F32)
        m_ref[...] = m_new

    sa_ref[...] = scores(0)

    def pair(jp, carry):
        first = 2 * jp
        sb_ref[...] = scores(first + 1)
        attend(sa_ref, first, causal=False)
        sa_ref[...] = scores(jnp.minimum(first + 2, n_tiles - 1))
        attend(sb_ref, first + 1, causal=True)
        return carry

    lax.fori_loop(0, n_tiles // 2, pair, 0)

    @pl.when(lax.rem(n_tiles, 2) == 1)
    def _():
        attend(sa_ref, n_tiles - 1, causal=True)

    inv_s = 1.0 / l_ref[...]
    o_sel = acc_ref[...]
    for h in range(r):
        cols = slice(h * dv, (h + 1) * dv)
        o_ref[:, cols] = (mix_ref[:, cols] + gated(o_sel, inv_s, zsil_ref[:, cols], 1, h)).astype(BF16)


def _overlap_matrix_t(n_blk, n_sel):
    a, b = SEL_LEN // CMP_STRIDE, CMP_LEN // CMP_STRIDE
    span = a + b - 2
    n = np.arange(n_blk)[None, :]
    j = np.arange(LANES)[:, None]
    diff = n - a * j
    ov = np.where((diff >= 0) & (diff <= span), np.minimum(np.minimum(diff, span - diff), min(a, b) - 1) + 1, 0)
    ov = np.where((n < n_blk - 1) & (j < n_sel), ov, 0)
    return jnp.asarray(ov, BF16)


def _nsa_attention(qp, zp, gate, kc, vc, kv, bsz, seq):
    g, r, dv, dkp, tq, tk = NSA_GROUPS, NSA_HPG, NSA_DV, NSA_DKP, ATT_TQ, SEL_TK
    nq = seq // tq
    n_blk = seq // CMP_STRIDE
    n_sel = seq // SEL_LEN
    ovt = _overlap_matrix_t(n_blk, n_sel)
    row = lambda b, gg, i: b * nq + i
    z_spec = lambda br: pl.BlockSpec((tq, r * dv), lambda b, gg, i: (row(b, gg, i), br * g + gg))
    ks0, vs0 = 0, g * dkp // dv
    kw0, vw0 = g * (dkp + dv) // dkp, (g * (dkp + dv) + g * dkp) // dv
    return pl.pallas_call(
        _nsa_attn_kernel,
        grid=(bsz, g, nq),
        in_specs=[pl.BlockSpec((tq, r * dkp), lambda b, gg, i: (row(b, gg, i), gg)),
                  z_spec(0), z_spec(1), z_spec(2),
                  pl.BlockSpec((tq, LANES), lambda b, gg, i: (row(b, gg, i), gg)),
                  pl.BlockSpec((1, 1, n_blk, dkp), lambda b, gg, i: (b, gg, 0, 0)),
                  pl.BlockSpec((1, 1, n_blk, dv), lambda b, gg, i: (b, gg, 0, 0)),
                  pl.BlockSpec(ovt.shape, lambda b, gg, i: (0, 0)),
                  pl.BlockSpec((seq, dkp), lambda b, gg, i: (b, ks0 + gg)),
                  pl.BlockSpec((seq, dv), lambda b, gg, i: (b, vs0 + gg)),
                  pl.BlockSpec((seq, dkp), lambda b, gg, i: (b, kw0 + gg)),
                  pl.BlockSpec((seq, dv), lambda b, gg, i: (b, vw0 + gg))],
        out_specs=pl.BlockSpec((tq, r * dv), lambda b, gg, i: (row(b, gg, i), gg)),
        out_shape=jax.ShapeDtypeStruct((bsz * seq, g * r * dv), BF16),
        scratch_shapes=[pltpu.VMEM((n_sel, tq), F32),
                        pltpu.VMEM((tk, r * tq), F32),
                        pltpu.VMEM((tk, r * tq), F32),
                        pltpu.VMEM((1, r * tq), F32),
                        pltpu.VMEM((1, r * tq), F32),
                        pltpu.VMEM((dv, r * tq), F32),
                        pltpu.VMEM((tq, r * dv), F32),
                        pltpu.VMEM((tq, r * dv), F32)],
        compiler_params=_params(3),
        name="nsa_attn",
    )(qp, zp, zp, zp, gate, kc, vc, ovt, kv, kv, kv, kv)


def _nsa_layer(x32, x16, shared, w_in, w_out, gain, bias, bsz, seq):
    kc, vc, kv = shared
    h, g, r, dk, dv, dkp = NSA_HEADS, NSA_GROUPS, NSA_HPG, NSA_DK, NSA_DV, NSA_DKP
    d = x16.shape[1]
    nq = h * dk
    nz = N_BRANCH * h * dv
    w_q = _pad_last(w_in[:, :nq].reshape(d, h, dk), dkp).reshape(d, h * dkp)
    w_gate = w_in[:, nq + nz:].reshape(d, N_BRANCH, g, r).transpose(0, 2, 1, 3).reshape(d, g, N_BRANCH * r)
    w_gate = _pad_last(w_gate, LANES).reshape(d, g * LANES)
    qp = _proj(x16, w_q, n_cols=h * dkp, col_block0=0, tm=512, tn=1024, out_dtype=BF16)
    zp = _proj(x16, w_in, n_cols=nz, col_block0=nq // 1024, tm=512, tn=1024, out_dtype=BF16)
    gate = _proj(x16, w_gate, n_cols=g * LANES, col_block0=0, tm=1024, tn=g * LANES, out_dtype=F32)
    mixed = _nsa_attention(qp, zp, gate, kc, vc, kv, bsz, seq)
    return _out_ln(mixed, w_out.astype(BF16), x32, gain, bias)


def kernel(x, positions, ret_w_in_0, ret_w_out_0, ln_g_0, ln_b_0, ret_w_in_1, ret_w_out_1, ln_g_1, ln_b_1,
           nsa_w_kv, nsa_pe_k, nsa_pe_v, nsa_w_ck1, nsa_w_ck2, nsa_w_cv1, nsa_w_cv2,
           nsa_w_in_2, nsa_w_out_2, ln_g_2, ln_b_2, nsa_w_in_3, nsa_w_out_3, ln_g_3, ln_b_3):
    bsz, seq, d = x.shape
    x32 = x.reshape(bsz * seq, d)
    x16 = x32.astype(BF16)
    cos, sin = _rope_tables(positions)
    x32, x16 = _retention_layer(x32, x16, cos, sin, ret_w_in_0, ret_w_out_0, ln_g_0, ln_b_0, bsz, seq)
    x32, x16 = _retention_layer(x32, x16, cos, sin, ret_w_in_1, ret_w_out_1, ln_g_1, ln_b_1, bsz, seq)
    shared = _nsa_shared_kv(x16, nsa_w_kv, nsa_pe_k, nsa_pe_v, nsa_w_ck1, nsa_w_ck2, nsa_w_cv1, nsa_w_cv2, bsz, seq)
    x32, x16 = _nsa_layer(x32, x16, shared, nsa_w_in_2, nsa_w_out_2, ln_g_2, ln_b_2, bsz, seq)
    x32, x16 = _nsa_layer(x32, x16, shared, nsa_w_in_3, nsa_w_out_3, ln_g_3, ln_b_3, bsz, seq)
    return x32.reshape(bsz, seq, d)
```

```python
import functools
import math

import numpy as np
import jax
import jax.numpy as jnp
from jax import lax
from jax.experimental import pallas as pl
from jax.experimental.pallas import tpu as pltpu

D_MODEL = 2048
DEPTH = 4
N_A_LAYERS = DEPTH // 2
RET_HEADS = 8
RET_DK = D_MODEL // RET_HEADS
RET_DV = 2 * D_MODEL // RET_HEADS
RET_CHUNK = 128
RET_THETA_BASE = 10000.0
NSA_HEADS = 16
NSA_GROUPS = 4
NSA_HPG = NSA_HEADS // NSA_GROUPS
NSA_DV = D_MODEL // NSA_HEADS
NSA_DK = 3 * NSA_DV // 2
NSA_DKP = 256
CMP_LEN = 32
CMP_STRIDE = 16
CMP_HID = 2 * NSA_DV
SEL_LEN = 64
SEL_TOPK = 16
WINDOW = 512
N_BRANCH = 3
ALPHA = (2.0 * DEPTH) ** 0.25
NEG_INF = -1e30
FORCE_SCORE = 1e9
LN_EPS = 1e-5

LANES = 128
SUBLANES = 8
ATT_TQ = 256
SEL_TK = 512
VMEM_LIMIT = 56 * 1024 * 1024

F32 = jnp.float32
BF16 = jnp.bfloat16
NT_DIMS = (((1,), (1,)), ((), ()))
TN_DIMS = (((0,), (0,)), ((), ()))


def _params(n_axes):
    return pltpu.CompilerParams(dimension_semantics=("arbitrary",) * n_axes, vmem_limit_bytes=VMEM_LIMIT)


def _sigmoid(x):
    return 0.5 * jnp.tanh(0.5 * x) + 0.5


def _silu(x):
    return x * _sigmoid(x)


def _proj_kernel(x_ref, w_ref, *rest):
    place_ref = rest[0] if len(rest) == 3 else None
    o_ref, wbf_ref = rest[-2:]

    @pl.when(pl.program_id(1) == 0)
    def _():
        if place_ref is None:
            wbf_ref[...] = w_ref[...].astype(BF16)
        else:
            b_in, b_out = place_ref.shape
            for c in range(w_ref.shape[1] // b_in):
                wbf_ref[:, c * b_out:(c + 1) * b_out] = jnp.dot(
                    w_ref[:, c * b_in:(c + 1) * b_in].astype(BF16), place_ref[...],
                    preferred_element_type=F32).astype(BF16)

    o_ref[...] = jnp.dot(x_ref[...].astype(BF16), wbf_ref[...], preferred_element_type=F32).astype(o_ref.dtype)


def _proj(x, w, *, n_tiles, col_block0, tm, tn, out_dtype, tn_in=None, place=None):
    m, k = x.shape
    tn_in = tn if tn_in is None else tn_in
    in_specs = [pl.BlockSpec((tm, k), lambda j, i: (i, 0)),
                pl.BlockSpec((k, tn_in), lambda j, i: (0, j + col_block0))]
    args = [x, w]
    if place is not None:
        in_specs.append(pl.BlockSpec(place.shape, lambda j, i: (0, 0)))
        args.append(place)
    return pl.pallas_call(
        _proj_kernel,
        grid=(n_tiles, m // tm),
        in_specs=in_specs,
        out_specs=pl.BlockSpec((tm, tn), lambda j, i: (i, j)),
        out_shape=jax.ShapeDtypeStruct((m, n_tiles * tn), out_dtype),
        scratch_shapes=[pltpu.VMEM((k, tn), BF16)],
        compiler_params=_params(2),
        name="proj",
    )(*args)


def _out_ln_kernel(a_ref, w_ref, x_ref, g_ref, b_ref, o_ref, acc_ref):
    kk = pl.program_id(1)

    @pl.when(kk == 0)
    def _():
        acc_ref[...] = jnp.zeros_like(acc_ref)

    acc_ref[...] += jnp.dot(a_ref[...], w_ref[...], preferred_element_type=F32)

    @pl.when(kk == pl.num_programs(1) - 1)
    def _():
        u = ALPHA * x_ref[...] + acc_ref[...]
        mu = jnp.mean(u, axis=-1, keepdims=True)
        var = jnp.mean(jnp.square(u - mu), axis=-1, keepdims=True)
        o_ref[...] = (u - mu) * lax.rsqrt(var + LN_EPS) * g_ref[...] + b_ref[...]


def _out_ln(a, w, x, gain, bias, *, tm=1024, tk=512):
    m, ka = a.shape
    d = w.shape[1]
    return pl.pallas_call(
        _out_ln_kernel,
        grid=(m // tm, ka // tk),
        in_specs=[pl.BlockSpec((tm, tk), lambda i, k: (i, k)),
                  pl.BlockSpec((tk, d), lambda i, k: (k, 0)),
                  pl.BlockSpec((tm, d), lambda i, k: (i, 0)),
                  pl.BlockSpec((1, d), lambda i, k: (0, 0)),
                  pl.BlockSpec((1, d), lambda i, k: (0, 0))],
        out_specs=pl.BlockSpec((tm, d), lambda i, k: (i, 0)),
        out_shape=jax.ShapeDtypeStruct((m, d), F32),
        scratch_shapes=[pltpu.VMEM((tm, d), F32)],
        compiler_params=_params(2),
        name="out_ln",
    )(a, w, x, gain.reshape(1, d), bias.reshape(1, d))


def _rope_kernel(pos_ref, freq_ref, cos_ref, sin_ref):
    ang = pos_ref[...].astype(F32) * freq_ref[...]
    cos_ref[...] = jnp.cos(ang)
    sin_ref[...] = jnp.sin(ang)


def _rope_tables(positions, tm=1024):
    t = positions.size
    half = RET_DK // 2
    inv_freq = 1.0 / (RET_THETA_BASE ** jnp.linspace(0.0, 1.0, half, dtype=F32))
    return pl.pallas_call(
        _rope_kernel,
        grid=(t // tm,),
        in_specs=[pl.BlockSpec((tm, 1), lambda i: (i, 0)), pl.BlockSpec((1, half), lambda i: (0, 0))],
        out_specs=[pl.BlockSpec((tm, half), lambda i: (i, 0))] * 2,
        out_shape=[jax.ShapeDtypeStruct((t, half), F32)] * 2,
        compiler_params=_params(1),
        name="rope_tables",
    )(positions.reshape(t, 1), inv_freq.reshape(1, half))


def _retention_kernel(q_ref, k_ref, v_ref, z_ref, cos_ref, sin_ref, dintra_ref, dq_ref, dk_ref, dch_ref,
                      o_ref, state_ref, *, n_sub):
    c = RET_CHUNK
    half = RET_DK // 2

    @pl.when(pl.program_id(2) == 0)
    def _():
        state_ref[...] = jnp.zeros_like(state_ref)

    def rotate(t, cosv, sinv):
        even, odd = t[:, :half], t[:, half:]
        return jnp.concatenate([even * cosv - odd * sinv, even * sinv + odd * cosv], axis=1)

    for j in range(n_sub):
        rows = pl.ds(j * c, c)
        cosv, sinv = cos_ref[rows, :], sin_ref[rows, :]
        qr = rotate(q_ref[rows, :].astype(F32), cosv, sinv)
        kr = rotate(k_ref[rows, :].astype(F32), cosv, sinv) * (RET_DK ** -0.5)
        qb, kb = qr.astype(BF16), kr.astype(BF16)
        kdb = (kr * dk_ref[0]).astype(BF16)
        v = v_ref[rows, :]
        state = state_ref[...]
        scores = lax.dot_general(qb, kb, NT_DIMS, preferred_element_type=F32) * dintra_ref[0]
        out = (jnp.dot(scores.astype(BF16), v, preferred_element_type=F32)
               + jnp.dot(qb, state.astype(BF16), preferred_element_type=F32) * dq_ref[0])
        state_ref[...] = state * dch_ref[0] + lax.dot_general(kdb, v, TN_DIMS, preferred_element_type=F32)
        mu = jnp.mean(out, axis=-1, keepdims=True)
        var = jnp.mean(jnp.square(out - mu), axis=-1, keepdims=True)
        normed = (out - mu) * lax.rsqrt(var + LN_EPS)
        o_ref[rows, :] = (normed * _silu(z_ref[rows, :].astype(F32))).astype(BF16)


def _retention_decays():
    h, c = RET_HEADS, RET_CHUNK
    log_gamma = jnp.log1p(-jnp.exp2(-5.0 - jnp.arange(h, dtype=F32)))
    idx = jnp.arange(c, dtype=F32)
    rel = idx[:, None] - idx[None, :]
    dintra = jnp.where(rel >= 0, jnp.exp(log_gamma[:, None, None] * jnp.maximum(rel, 0.0)), 0.0)
    dq = jnp.exp(log_gamma[:, None] * (idx + 1.0))
    dk = jnp.exp(log_gamma[:, None] * (c - 1.0 - idx))
    dch = jnp.exp(log_gamma * c)
    return (dintra,
            jnp.broadcast_to(dq[:, :, None], (h, c, RET_DV)),
            jnp.broadcast_to(dk[:, :, None], (h, c, RET_DK)),
            jnp.broadcast_to(dch[:, None, None], (h, 1, RET_DV)))


def _retention(qk, vz, cos, sin, bsz, seq, *, ts=512):
    h, dk, dv, c = RET_HEADS, RET_DK, RET_DV, RET_CHUNK
    ns = seq // ts
    dintra, dq, dkt, dch = _retention_decays()
    row = lambda b, hh, s: b * ns + s
    return pl.pallas_call(
        functools.partial(_retention_kernel, n_sub=ts // c),
        grid=(bsz, h, ns),
        in_specs=[pl.BlockSpec((ts, dk), lambda b, hh, s: (row(b, hh, s), hh)),
                  pl.BlockSpec((ts, dk), lambda b, hh, s: (row(b, hh, s), h + hh)),
                  pl.BlockSpec((ts, dv), lambda b, hh, s: (row(b, hh, s), hh)),
                  pl.BlockSpec((ts, dv), lambda b, hh, s: (row(b, hh, s), h + hh)),
                  pl.BlockSpec((ts, dk // 2), lambda b, hh, s: (row(b, hh, s), 0)),
                  pl.BlockSpec((ts, dk // 2), lambda b, hh, s: (row(b, hh, s), 0)),
                  pl.BlockSpec((1, c, c), lambda b, hh, s: (hh, 0, 0)),
                  pl.BlockSpec((1, c, dv), lambda b, hh, s: (hh, 0, 0)),
                  pl.BlockSpec((1, c, dk), lambda b, hh, s: (hh, 0, 0)),
                  pl.BlockSpec((1, 1, dv), lambda b, hh, s: (hh, 0, 0))],
        out_specs=pl.BlockSpec((ts, dv), lambda b, hh, s: (row(b, hh, s), hh)),
        out_shape=jax.ShapeDtypeStruct((bsz * seq, h * dv), BF16),
        scratch_shapes=[pltpu.VMEM((dk, dv), F32)],
        compiler_params=_params(3),
        name="retention",
    )(qk, qk, vz, vz, cos, sin, dintra, dq, dkt, dch)


def _deinterleave_matrix(width):
    p = np.zeros((width, width), np.float32)
    j = np.arange(width // 2)
    p[2 * j, j] = 1.0
    p[2 * j + 1, width // 2 + j] = 1.0
    return jnp.asarray(p, BF16)


def _retention_layer(x, cos, sin, w_in, w_out, gain, bias, bsz, seq):
    h, dk, dv = RET_HEADS, RET_DK, RET_DV
    tn = 1024
    qk = _proj(x, w_in, n_tiles=2 * h * dk // tn, col_block0=0, tm=512, tn=tn, out_dtype=BF16,
               place=_deinterleave_matrix(dk))
    vz = _proj(x, w_in, n_tiles=2 * h * dv // tn, col_block0=2 * h * dk // tn, tm=512, tn=tn, out_dtype=BF16)
    mixed = _retention(qk, vz, cos, sin, bsz, seq)
    return _out_ln(mixed, w_out.astype(BF16), x, gain, bias)


def _compress_kernel(*refs):
    x_refs, (pe_ref, w1_ref, w2_ref, o_ref) = refs[:-4], refs[-4:]
    n_blk = o_ref.shape[2]
    half = CMP_LEN // 2
    acc_lo = jnp.zeros((n_blk, CMP_HID), F32)
    acc_hi = jnp.zeros((n_blk, CMP_HID), F32)
    for l in range(half):
        rows = jnp.concatenate([x_ref[pl.ds(l, n_blk, stride=CMP_STRIDE), :] for x_ref in x_refs], axis=1)
        acc_lo += jnp.dot((rows + pe_ref[l:l + 1, :]).astype(BF16), w1_ref[l], preferred_element_type=F32)
        acc_hi += jnp.dot((rows + pe_ref[half + l:half + l + 1, :]).astype(BF16), w1_ref[half + l],
                          preferred_element_type=F32)
    hid = acc_lo + pltpu.roll(acc_hi, shift=n_blk - 1, axis=0)
    out = jnp.dot(_silu(hid).astype(BF16), w2_ref[...], preferred_element_type=F32)
    keep = lax.broadcasted_iota(jnp.int32, out.shape, 0) < n_blk - 1
    o_ref[0, 0] = jnp.where(keep, out, 0.0).astype(BF16)


def _compress(kvc, pe, w1, w2, bsz, seq, *, width, col_block0):
    g = NSA_GROUPS
    n_blk = seq // CMP_STRIDE
    n_slabs = width // LANES
    slab = lambda j: pl.BlockSpec((seq, LANES), lambda b, gg: (b, (col_block0 + gg) * n_slabs + j))
    return pl.pallas_call(
        _compress_kernel,
        grid=(bsz, g),
        in_specs=[slab(j) for j in range(n_slabs)] + [
                  pl.BlockSpec(pe.shape, lambda b, gg: (0, 0)),
                  pl.BlockSpec(w1.shape, lambda b, gg: (0, 0, 0)),
                  pl.BlockSpec(w2.shape, lambda b, gg: (0, 0))],
        out_specs=pl.BlockSpec((1, 1, n_blk, w2.shape[1]), lambda b, gg: (b, gg, 0, 0)),
        out_shape=jax.ShapeDtypeStruct((bsz, g, n_blk, w2.shape[1]), BF16),
        compiler_params=_params(2),
        name="compress",
    )(*([kvc] * n_slabs), pe, w1, w2)


def _pad_last(a, width):
    return jnp.pad(a, [(0, 0)] * (a.ndim - 1) + [(0, width - a.shape[-1])])


def _head_pad_matrix(n_heads, width, padded, tail):
    p = np.zeros((n_heads * width + tail, n_heads * padded + tail), np.float32)
    c = np.arange(n_heads * width)
    p[c, (c // width) * padded + c % width] = 1.0
    t = np.arange(tail)
    p[n_heads * width + t, n_heads * padded + t] = 1.0
    return jnp.asarray(p, BF16)


def _nsa_shared_kv(x, w_kv, pe_k, pe_v, w_ck1, w_ck2, w_cv1, w_cv2, bsz, seq):
    g, dk, dv, dkp = NSA_GROUPS, NSA_DK, NSA_DV, NSA_DKP
    place = _head_pad_matrix(g, dk, dkp, g * dv)
    tn_in, tn = place.shape
    kvc = _proj(x, w_kv, n_tiles=1, col_block0=0, tm=512, tn=tn, tn_in=tn_in, out_dtype=F32, place=place)
    kv = _proj(x, w_kv, n_tiles=2, col_block0=1, tm=512, tn=tn, tn_in=tn_in, out_dtype=BF16, place=place)
    w1k = _pad_last(w_ck1.reshape(CMP_LEN, dk, CMP_HID).transpose(0, 2, 1), dkp).transpose(0, 2, 1).astype(BF16)
    w1v = w_cv1.reshape(CMP_LEN, dv, CMP_HID).astype(BF16)
    kc = _compress(kvc, _pad_last(pe_k, dkp), w1k, _pad_last(w_ck2, dkp).astype(BF16), bsz, seq,
                   width=dkp, col_block0=0)
    vc = _compress(kvc, pe_v, w1v, w_cv2.astype(BF16), bsz, seq, width=dv, col_block0=g * dkp // dv)
    return kc, vc, kv


def _nsa_attn_kernel(q_ref, zc_ref, zs_ref, zw_ref, gate_ref, kc_ref, vc_ref, ovt_ref,
                     ks_ref, vs_ref, kw_ref, vw_ref, o_ref,
                     bias_ref, sa_ref, sb_ref, m_ref, l_ref, acc_ref, mix_ref, zsil_ref):
    tq, r, dv, tk = q_ref.shape[0], NSA_HPG, NSA_DV, SEL_TK
    lanes = r * tq
    n_blk = kc_ref.shape[2]
    n_sel = n_blk * CMP_STRIDE // SEL_LEN
    c_exp = (NSA_DK ** -0.5) * math.log2(math.e)
    q0 = pl.program_id(2) * tq
    q4 = jnp.concatenate([q_ref[:, h * NSA_DKP:(h + 1) * NSA_DKP] for h in range(r)], axis=0)

    def qpos(shape):
        return q0 + lax.rem(lax.broadcasted_iota(jnp.int32, shape, 1), tq)

    def softmax_terms(s):
        e = jnp.exp2((s - jnp.max(s, axis=0, keepdims=True)) * c_exp)
        return e, jnp.sum(e, axis=0, keepdims=True)

    def all_heads(tile):
        return jnp.concatenate([tile] * r, axis=1)

    gates_t = _sigmoid(gate_ref[...]).T

    def gated(o_t, inv_l, silu_z, br, h):
        cols = slice(h * tq, (h + 1) * tq)
        row_scale = gates_t[br * r + h:br * r + h + 1, :] * inv_l[:, cols]
        return (o_t[:, cols] * row_scale).T * silu_z

    s = lax.dot_general(kc_ref[0, 0], q4, NT_DIMS, preferred_element_type=F32)
    blk_end = lax.broadcasted_iota(jnp.int32, (n_blk, tq), 0) * CMP_STRIDE + (CMP_LEN - 1)
    s = s + all_heads(jnp.where(blk_end <= qpos((n_blk, tq)), 0.0, NEG_INF))
    e, l = softmax_terms(s)
    inv_c = jnp.where(qpos((1, lanes)) >= CMP_LEN - 1, 1.0 / l, 0.0)
    eb = e.astype(BF16)
    o_cmp = lax.dot_general(vc_ref[0, 0], eb, TN_DIMS, preferred_element_type=F32)

    imp = sum(jnp.dot(ovt_ref[...], eb[:, h * tq:(h + 1) * tq], preferred_element_type=F32)
              * inv_c[:, h * tq:(h + 1) * tq] for h in range(r))
    imp = imp[:n_sel, :]
    blk = lax.broadcasted_iota(jnp.int32, imp.shape, 0)
    cur = (q0 + lax.broadcasted_iota(jnp.int32, imp.shape, 1)) // SEL_LEN
    forced = (blk == 0) | (blk == cur) | (blk == cur - 1)
    score = jnp.where(forced, FORCE_SCORE, jnp.where(blk <= cur, imp, -1.0))
    groups = [score[g0:g0 + SUBLANES, :] for g0 in range(0, n_sel, SUBLANES)]
    ranks = [jnp.zeros((SUBLANES, tq), F32) for _ in groups]
    for i in range(n_sel):
        row = score[i:i + 1, :]
        for gi, grp in enumerate(groups):
            g0 = gi * SUBLANES
            if g0 > i:
                ahead = row >= grp
            elif g0 + SUBLANES - 1 < i:
                ahead = row > grp
            else:
                after = lax.broadcasted_iota(jnp.int32, grp.shape, 0) + g0 > i
                ahead = (row > grp) | (after & (row == grp))
            ranks[gi] = ranks[gi] + jnp.where(ahead, 1.0, 0.0)
    rank = jnp.concatenate(ranks, axis=0)
    bias_ref[...] = jnp.where(rank < float(min(SEL_TOPK, n_sel)), 0.0, NEG_INF)

    span = WINDOW + tq
    k0 = pl.multiple_of(jnp.maximum(q0 - WINDOW, 0), tq)
    s = lax.dot_general(kw_ref[pl.ds(k0, span), :], q4, NT_DIMS, preferred_element_type=F32)
    dist = qpos((span, tq)) - (k0 + lax.broadcasted_iota(jnp.int32, (span, tq), 0))
    s = s + all_heads(jnp.where((dist >= 0) & (dist < WINDOW), 0.0, NEG_INF))
    e, l = softmax_terms(s)
    inv_w = 1.0 / l
    o_win = lax.dot_general(vw_ref[pl.ds(k0, span), :], e.astype(BF16), TN_DIMS, preferred_element_type=F32)

    for h in range(r):
        cols = slice(h * dv, (h + 1) * dv)
        mix_ref[:, cols] = (gated(o_cmp, inv_c, _silu(zc_ref[:, cols].astype(F32)), 0, h)
                            + gated(o_win, inv_w, _silu(zw_ref[:, cols].astype(F32)), 2, h))
    zsil_ref[...] = _silu(zs_ref[...].astype(F32))

    m_ref[...] = jnp.full(m_ref.shape, NEG_INF, F32)
    l_ref[...] = jnp.zeros_like(l_ref)
    acc_ref[...] = jnp.zeros_like(acc_ref)
    blocks_per_tile = tk // SEL_LEN
    n_tiles = q0 // tk + 1
    causal_gap = (q0 + lax.broadcasted_iota(jnp.int32, (tk, tq), 1)) - lax.broadcasted_iota(jnp.int32, (tk, tq), 0)

    def scores(tile):
        kbase = pl.multiple_of(tile * tk, tk)
        return lax.dot_general(ks_ref[pl.ds(kbase, tk), :], q4, NT_DIMS, preferred_element_type=F32)

    def attend(s_ref, tile, causal):
        kbase = pl.multiple_of(tile * tk, tk)
        rows = bias_ref[pl.ds(pl.multiple_of(tile * blocks_per_tile, blocks_per_tile), blocks_per_tile), :]
        bias = jnp.concatenate([jnp.broadcast_to(rows[j:j + 1, :], (SEL_LEN, tq)) for j in range(blocks_per_tile)],
                               axis=0)
        if causal:
            bias = jnp.where(causal_gap >= kbase, bias, NEG_INF)
        s = s_ref[...] + all_heads(bias)
        m_prev = m_ref[...]
        m_new = jnp.maximum(m_prev, jnp.max(s, axis=0, keepdims=True))
        alpha = jnp.exp2((m_prev - m_new) * c_exp)
        pt = jnp.exp2((s - m_new) * c_exp)
        l_ref[...] = alpha * l_ref[...] + jnp.sum(pt, axis=0, keepdims=True)
        acc_ref[...] = alpha * acc_ref[...] + lax.dot_general(vs_ref[pl.ds(kbase, tk), :], pt.astype(BF16), TN_DIMS,
                                                             preferred_element_type=F32)
        m_ref[...] = m_new

    sa_ref[...] = scores(0)

    def pair(jp, carry):
        first = 2 * jp
        sb_ref[...] = scores(first + 1)
        attend(sa_ref, first, causal=False)
        sa_ref[...] = scores(jnp.minimum(first + 2, n_tiles - 1))
        attend(sb_ref, first + 1, causal=True)
        return carry

    lax.fori_loop(0, n_tiles // 2, pair, 0)

    @pl.when(lax.rem(n_tiles, 2) == 1)
    def _():
        attend(sa_ref, n_tiles - 1, causal=True)

    inv_s = 1.0 / l_ref[...]
    o_sel = acc_ref[...]
    for h in range(r):
        cols = slice(h * dv, (h + 1) * dv)
        o_ref[:, cols] = (mix_ref[:, cols] + gated(o_sel, inv_s, zsil_ref[:, cols], 1, h)).astype(BF16)


def _overlap_matrix_t(n_blk, n_sel):
    a, b = SEL_LEN // CMP_STRIDE, CMP_LEN // CMP_STRIDE
    span = a + b - 2
    n = np.arange(n_blk)[None, :]
    j = np.arange(LANES)[:, None]
    diff = n - a * j
    ov = np.where((diff >= 0) & (diff <= span), np.minimum(np.minimum(diff, span - diff), min(a, b) - 1) + 1, 0)
    ov = np.where((n < n_blk - 1) & (j < n_sel), ov, 0)
    return jnp.asarray(ov, BF16)


def _nsa_attention(qp, zp, gate, kc, vc, kv, bsz, seq):
    g, r, dv, dkp, tq, tk = NSA_GROUPS, NSA_HPG, NSA_DV, NSA_DKP, ATT_TQ, SEL_TK
    nq = seq // tq
    n_blk = seq // CMP_STRIDE
    n_sel = seq // SEL_LEN
    ovt = _overlap_matrix_t(n_blk, n_sel)
    row = lambda b, gg, i: b * nq + i
    z_spec = lambda br: pl.BlockSpec((tq, r * dv), lambda b, gg, i: (row(b, gg, i), br * g + gg))
    ks0, vs0 = 0, g * dkp // dv
    kw0, vw0 = g * (dkp + dv) // dkp, (g * (dkp + dv) + g * dkp) // dv
    return pl.pallas_call(
        _nsa_attn_kernel,
        grid=(bsz, g, nq),
        in_specs=[pl.BlockSpec((tq, r * dkp), lambda b, gg, i: (row(b, gg, i), gg)),
                  z_spec(0), z_spec(1), z_spec(2),
                  pl.BlockSpec((tq, LANES), lambda b, gg, i: (row(b, gg, i), gg)),
                  pl.BlockSpec((1, 1, n_blk, dkp), lambda b, gg, i: (b, gg, 0, 0)),
                  pl.BlockSpec((1, 1, n_blk, dv), lambda b, gg, i: (b, gg, 0, 0)),
                  pl.BlockSpec(ovt.shape, lambda b, gg, i: (0, 0)),
                  pl.BlockSpec((seq, dkp), lambda b, gg, i: (b, ks0 + gg)),
                  pl.BlockSpec((seq, dv), lambda b, gg, i: (b, vs0 + gg)),
                  pl.BlockSpec((seq, dkp), lambda b, gg, i: (b, kw0 + gg)),
                  pl.BlockSpec((seq, dv), lambda b, gg, i: (b, vw0 + gg))],
        out_specs=pl.BlockSpec((tq, r * dv), lambda b, gg, i: (row(b, gg, i), gg)),
        out_shape=jax.ShapeDtypeStruct((bsz * seq, g * r * dv), BF16),
        scratch_shapes=[pltpu.VMEM((n_sel, tq), F32),
                        pltpu.VMEM((tk, r * tq), F32),
                        pltpu.VMEM((tk, r * tq), F32),
                        pltpu.VMEM((1, r * tq), F32),
                        pltpu.VMEM((1, r * tq), F32),
                        pltpu.VMEM((dv, r * tq), F32),
                        pltpu.VMEM((tq, r * dv), F32),
                        pltpu.VMEM((tq, r * dv), F32)],
        compiler_params=_params(3),
        name="nsa_attn",
    )(qp, zp, zp, zp, gate, kc, vc, ovt, kv, kv, kv, kv)


def _nsa_layer(x, shared, w_in, w_out, gain, bias, bsz, seq):
    kc, vc, kv = shared
    h, g, r, dk, dv, dkp = NSA_HEADS, NSA_GROUPS, NSA_HPG, NSA_DK, NSA_DV, NSA_DKP
    d = x.shape[1]
    nq = h * dk
    nz = N_BRANCH * h * dv
    tn = 1024
    w_gate = w_in[:, nq + nz:].reshape(d, N_BRANCH, g, r).transpose(0, 2, 1, 3).reshape(d, g, N_BRANCH * r)
    w_gate = _pad_last(w_gate, LANES).reshape(d, g * LANES)
    qp = _proj(x, w_in, n_tiles=g, col_block0=0, tm=512, tn=r * dkp, tn_in=r * dk, out_dtype=BF16,
               place=_head_pad_matrix(r, dk, dkp, 0))
    zp = _proj(x, w_in, n_tiles=nz // tn, col_block0=nq // tn, tm=512, tn=tn, out_dtype=BF16)
    gate = _proj(x, w_gate, n_tiles=1, col_block0=0, tm=1024, tn=g * LANES, out_dtype=F32)
    mixed = _nsa_attention(qp, zp, gate, kc, vc, kv, bsz, seq)
    return _out_ln(mixed, w_out.astype(BF16), x, gain, bias)


def kernel(x, positions, ret_w_in_0, ret_w_out_0, ln_g_0, ln_b_0, ret_w_in_1, ret_w_out_1, ln_g_1, ln_b_1,
           nsa_w_kv, nsa_pe_k, nsa_pe_v, nsa_w_ck1, nsa_w_ck2, nsa_w_cv1, nsa_w_cv2,
           nsa_w_in_2, nsa_w_out_2, ln_g_2, ln_b_2, nsa_w_in_3, nsa_w_out_3, ln_g_3, ln_b_3):
    bsz, seq, d = x.shape
    h = x.reshape(bsz * seq, d)
    cos, sin = _rope_tables(positions)
    h = _retention_layer(h, cos, sin, ret_w_in_0, ret_w_out_0, ln_g_0, ln_b_0, bsz, seq)
    h = _retention_layer(h, cos, sin, ret_w_in_1, ret_w_out_1, ln_g_1, ln_b_1, bsz, seq)
    shared = _nsa_shared_kv(h, nsa_w_kv, nsa_pe_k, nsa_pe_v, nsa_w_ck1, nsa_w_ck2, nsa_w_cv1, nsa_w_cv2, bsz, seq)
    h = _nsa_layer(h, shared, nsa_w_in_2, nsa_w_out_2, ln_g_2, ln_b_2, bsz, seq)
    h = _nsa_layer(h, shared, nsa_w_in_3, nsa_w_out_3, ln_g_3, ln_b_3, bsz, seq)
    return h.reshape(bsz, seq, d)
```

```python
import functools
import math

import numpy as np
import jax
import jax.numpy as jnp
from jax import lax
from jax.experimental import pallas as pl
from jax.experimental.pallas import tpu as pltpu

D_MODEL = 2048
DEPTH = 4
N_A_LAYERS = DEPTH // 2
RET_HEADS = 8
RET_DK = D_MODEL // RET_HEADS
RET_DV = 2 * D_MODEL // RET_HEADS
RET_CHUNK = 128
RET_THETA_BASE = 10000.0
NSA_HEADS = 16
NSA_GROUPS = 4
NSA_HPG = NSA_HEADS // NSA_GROUPS
NSA_DV = D_MODEL // NSA_HEADS
NSA_DK = 3 * NSA_DV // 2
NSA_DKP = 256
CMP_LEN = 32
CMP_STRIDE = 16
CMP_HID = 2 * NSA_DV
SEL_LEN = 64
SEL_TOPK = 16
WINDOW = 512
N_BRANCH = 3
ALPHA = (2.0 * DEPTH) ** 0.25
NEG_INF = -1e30
FORCE_SCORE = 1e9
LN_EPS = 1e-5

LANES = 128
SUBLANES = 8
BF16_SUBLANES = 16
ATT_TQ = 256
SEL_TK = 512
V_ROWS = NSA_DV + BF16_SUBLANES
VMEM_LIMIT = 56 * 1024 * 1024

F32 = jnp.float32
BF16 = jnp.bfloat16
NT_DIMS = (((1,), (1,)), ((), ()))
TN_DIMS = (((0,), (0,)), ((), ()))


def _params(n_axes):
    return pltpu.CompilerParams(dimension_semantics=("arbitrary",) * n_axes, vmem_limit_bytes=VMEM_LIMIT)


def _sigmoid(x):
    return 0.5 * jnp.tanh(0.5 * x) + 0.5


def _silu(x):
    half = 0.5 * x
    return half + half * jnp.tanh(half)


def _proj_kernel(x_ref, w_ref, *rest, out_scale, valid_cols, w_transposed):
    place_ref = rest[0] if len(rest) == 3 else None
    o_ref, wbf_ref = rest[-2:]
    col_axis = 0 if w_transposed else 1

    @pl.when(pl.program_id(1) == 0)
    def _():
        if place_ref is None:
            wbf_ref[...] = w_ref[...].astype(BF16)
        else:
            b_in, b_out = place_ref.shape[::-1] if w_transposed else place_ref.shape
            for c in range(w_ref.shape[col_axis] // b_in):
                src, dst = slice(c * b_in, (c + 1) * b_in), slice(c * b_out, (c + 1) * b_out)
                w = w_ref[src, :] if w_transposed else w_ref[:, src]
                if valid_cols is not None:
                    w = jnp.where(lax.broadcasted_iota(jnp.int32, w.shape, col_axis) < valid_cols, w, 0.0)
                if w_transposed:
                    wbf_ref[dst, :] = jnp.dot(place_ref[...], w.astype(BF16), preferred_element_type=F32).astype(BF16)
                else:
                    wbf_ref[:, dst] = jnp.dot(w.astype(BF16), place_ref[...], preferred_element_type=F32).astype(BF16)

    xb = x_ref[...].astype(BF16)
    if w_transposed:
        acc = lax.dot_general(xb, wbf_ref[...], NT_DIMS, preferred_element_type=F32)
    else:
        acc = jnp.dot(xb, wbf_ref[...], preferred_element_type=F32)
    o_ref[...] = (acc if out_scale is None else acc * out_scale).astype(o_ref.dtype)


def _proj(x, w, *, n_tiles, col_block0, tm, tn, out_dtype, tn_in=None, place=None, out_scale=None, valid_cols=None,
          w_transposed=False):
    m, k = x.shape
    tn_in = tn if tn_in is None else tn_in
    if w_transposed:
        w_spec = pl.BlockSpec((tn_in, k), lambda j, i: (j + col_block0, 0))
    else:
        w_spec = pl.BlockSpec((k, tn_in), lambda j, i: (0, j + col_block0))
    in_specs = [pl.BlockSpec((tm, k), lambda j, i: (i, 0)), w_spec]
    args = [x, w]
    if place is not None:
        in_specs.append(pl.BlockSpec(place.shape, lambda j, i: (0, 0)))
        args.append(place)
    return pl.pallas_call(
        functools.partial(_proj_kernel, out_scale=out_scale, valid_cols=valid_cols, w_transposed=w_transposed),
        grid=(n_tiles, m // tm),
        in_specs=in_specs,
        out_specs=pl.BlockSpec((tm, tn), lambda j, i: (i, j)),
        out_shape=jax.ShapeDtypeStruct((m, n_tiles * tn), out_dtype),
        scratch_shapes=[pltpu.VMEM((tn, k) if w_transposed else (k, tn), BF16)],
        compiler_params=_params(2),
        name="proj",
    )(*args)


def _out_ln_kernel(a_ref, w_ref, x_ref, g_ref, b_ref, o_ref, acc_ref):
    kk = pl.program_id(1)

    @pl.when(kk == 0)
    def _():
        acc_ref[...] = jnp.zeros_like(acc_ref)

    acc_ref[...] += jnp.dot(a_ref[...], w_ref[...], preferred_element_type=F32)

    @pl.when(kk == pl.num_programs(1) - 1)
    def _():
        u = ALPHA * x_ref[...] + acc_ref[...]
        mu = jnp.mean(u, axis=-1, keepdims=True)
        var = jnp.mean(jnp.square(u - mu), axis=-1, keepdims=True)
        o_ref[...] = (u - mu) * lax.rsqrt(var + LN_EPS) * g_ref[...] + b_ref[...]


def _out_ln(a, w, x, gain, bias, *, tm=1024, tk=512):
    m, ka = a.shape
    d = w.shape[1]
    return pl.pallas_call(
        _out_ln_kernel,
        grid=(m // tm, ka // tk),
        in_specs=[pl.BlockSpec((tm, tk), lambda i, k: (i, k)),
                  pl.BlockSpec((tk, d), lambda i, k: (k, 0)),
                  pl.BlockSpec((tm, d), lambda i, k: (i, 0)),
                  pl.BlockSpec((1, d), lambda i, k: (0, 0)),
                  pl.BlockSpec((1, d), lambda i, k: (0, 0))],
        out_specs=pl.BlockSpec((tm, d), lambda i, k: (i, 0)),
        out_shape=jax.ShapeDtypeStruct((m, d), F32),
        scratch_shapes=[pltpu.VMEM((tm, d), F32)],
        compiler_params=_params(2),
        name="out_ln",
    )(a, w, x, gain.reshape(1, d), bias.reshape(1, d))


def _rope_kernel(pos_ref, freq_ref, cos_ref, sin_ref):
    ang = pos_ref[...].astype(F32) * freq_ref[...]
    cos_ref[...] = jnp.cos(ang)
    sin_ref[...] = jnp.sin(ang)


def _rope_tables(positions, tm=1024):
    t = positions.size
    half = RET_DK // 2
    inv_freq = 1.0 / (RET_THETA_BASE ** jnp.linspace(0.0, 1.0, half, dtype=F32))
    return pl.pallas_call(
        _rope_kernel,
        grid=(t // tm,),
        in_specs=[pl.BlockSpec((tm, 1), lambda i: (i, 0)), pl.BlockSpec((1, half), lambda i: (0, 0))],
        out_specs=[pl.BlockSpec((tm, half), lambda i: (i, 0))] * 2,
        out_shape=[jax.ShapeDtypeStruct((t, half), F32)] * 2,
        compiler_params=_params(1),
        name="rope_tables",
    )(positions.reshape(t, 1), inv_freq.reshape(1, half))


def _retention_kernel(q_ref, k_ref, v_ref, z_ref, cos_ref, sin_ref, dintra_ref, dq_ref, dk_ref, dch_ref,
                      o_ref, state_ref, *, n_sub):
    c = RET_CHUNK
    half = RET_DK // 2

    @pl.when(pl.program_id(2) == 0)
    def _():
        state_ref[...] = jnp.zeros_like(state_ref)

    def rotate(t, cosv, sinv):
        even, odd = t[:, :half], t[:, half:]
        return jnp.concatenate([even * cosv - odd * sinv, even * sinv + odd * cosv], axis=1)

    for j in range(n_sub):
        rows = pl.ds(j * c, c)
        cosv, sinv = cos_ref[rows, :], sin_ref[rows, :]
        qr = rotate(q_ref[rows, :].astype(F32), cosv, sinv)
        kr = rotate(k_ref[rows, :].astype(F32), cosv, sinv) * (RET_DK ** -0.5)
        qb, kb = qr.astype(BF16), kr.astype(BF16)
        kdb = (kr * dk_ref[0]).astype(BF16)
        v = v_ref[rows, :]
        state = state_ref[...]
        scores = lax.dot_general(qb, kb, NT_DIMS, preferred_element_type=F32) * dintra_ref[0]
        out = (jnp.dot(scores.astype(BF16), v, preferred_element_type=F32)
               + jnp.dot(qb, state.astype(BF16), preferred_element_type=F32) * dq_ref[0])
        state_ref[...] = state * dch_ref[0] + lax.dot_general(kdb, v, TN_DIMS, preferred_element_type=F32)
        mu = jnp.mean(out, axis=-1, keepdims=True)
        var = jnp.mean(jnp.square(out - mu), axis=-1, keepdims=True)
        normed = (out - mu) * lax.rsqrt(var + LN_EPS)
        o_ref[rows, :] = (normed * _silu(z_ref[rows, :].astype(F32))).astype(BF16)


def _retention_decays():
    h, c = RET_HEADS, RET_CHUNK
    log_gamma = jnp.log1p(-jnp.exp2(-5.0 - jnp.arange(h, dtype=F32)))
    idx = jnp.arange(c, dtype=F32)
    rel = idx[:, None] - idx[None, :]
    dintra = jnp.where(rel >= 0, jnp.exp(log_gamma[:, None, None] * jnp.maximum(rel, 0.0)), 0.0)
    dq = jnp.exp(log_gamma[:, None] * (idx + 1.0))
    dk = jnp.exp(log_gamma[:, None] * (c - 1.0 - idx))
    dch = jnp.exp(log_gamma * c)
    return (dintra,
            jnp.broadcast_to(dq[:, :, None], (h, c, RET_DV)),
            jnp.broadcast_to(dk[:, :, None], (h, c, RET_DK)),
            jnp.broadcast_to(dch[:, None, None], (h, 1, RET_DV)))


def _retention(qk, vz, cos, sin, bsz, seq, *, ts=512):
    h, dk, dv, c = RET_HEADS, RET_DK, RET_DV, RET_CHUNK
    ns = seq // ts
    dintra, dq, dkt, dch = _retention_decays()
    row = lambda b, hh, s: b * ns + s
    return pl.pallas_call(
        functools.partial(_retention_kernel, n_sub=ts // c),
        grid=(bsz, h, ns),
        in_specs=[pl.BlockSpec((ts, dk), lambda b, hh, s: (row(b, hh, s), hh)),
                  pl.BlockSpec((ts, dk), lambda b, hh, s: (row(b, hh, s), h + hh)),
                  pl.BlockSpec((ts, dv), lambda b, hh, s: (row(b, hh, s), hh)),
                  pl.BlockSpec((ts, dv), lambda b, hh, s: (row(b, hh, s), h + hh)),
                  pl.BlockSpec((ts, dk // 2), lambda b, hh, s: (row(b, hh, s), 0)),
                  pl.BlockSpec((ts, dk // 2), lambda b, hh, s: (row(b, hh, s), 0)),
                  pl.BlockSpec((1, c, c), lambda b, hh, s: (hh, 0, 0)),
                  pl.BlockSpec((1, c, dv), lambda b, hh, s: (hh, 0, 0)),
                  pl.BlockSpec((1, c, dk), lambda b, hh, s: (hh, 0, 0)),
                  pl.BlockSpec((1, 1, dv), lambda b, hh, s: (hh, 0, 0))],
        out_specs=pl.BlockSpec((ts, dv), lambda b, hh, s: (row(b, hh, s), hh)),
        out_shape=jax.ShapeDtypeStruct((bsz * seq, h * dv), BF16),
        scratch_shapes=[pltpu.VMEM((dk, dv), F32)],
        compiler_params=_params(3),
        name="retention",
    )(qk, qk, vz, vz, cos, sin, dintra, dq, dkt, dch)


def _deinterleave_matrix(width):
    p = np.zeros((width, width), np.float32)
    j = np.arange(width // 2)
    p[2 * j, j] = 1.0
    p[2 * j + 1, width // 2 + j] = 1.0
    return jnp.asarray(p, BF16)


def _retention_layer(x, cos, sin, w_in, w_out, gain, bias, bsz, seq):
    h, dk, dv = RET_HEADS, RET_DK, RET_DV
    tn = 1024
    qk = _proj(x, w_in, n_tiles=2 * h * dk // tn, col_block0=0, tm=1024, tn=tn, out_dtype=BF16,
               place=_deinterleave_matrix(dk))
    vz = _proj(x, w_in, n_tiles=2 * h * dv // tn, col_block0=2 * h * dk // tn, tm=1024, tn=tn, out_dtype=BF16)
    mixed = _retention(qk, vz, cos, sin, bsz, seq)
    return _out_ln(mixed, w_out.astype(BF16), x, gain, bias)


def _compress_kernel(*refs, transposed):
    x_refs, (pe_ref, w1_ref, w2_ref, o_ref) = refs[:-4], refs[-4:]
    n_blk = o_ref.shape[3] if transposed else o_ref.shape[2]
    half = CMP_LEN // 2
    acc_lo = jnp.zeros((n_blk, CMP_HID), F32)
    acc_hi = jnp.zeros((n_blk, CMP_HID), F32)
    for l in range(half):
        rows = jnp.concatenate([x_ref[pl.ds(l, n_blk, stride=CMP_STRIDE), :] for x_ref in x_refs], axis=1)
        acc_lo += jnp.dot((rows + pe_ref[l:l + 1, :]).astype(BF16), w1_ref[l], preferred_element_type=F32)
        acc_hi += jnp.dot((rows + pe_ref[half + l:half + l + 1, :]).astype(BF16), w1_ref[half + l],
                          preferred_element_type=F32)
    hid = acc_lo + pltpu.roll(acc_hi, shift=n_blk - 1, axis=0)
    out = jnp.dot(_silu(hid).astype(BF16), w2_ref[...], preferred_element_type=F32)
    keep = lax.broadcasted_iota(jnp.int32, out.shape, 0) < n_blk - 1
    out = jnp.where(keep, out, 0.0)
    if transposed:
        width = out.shape[1]
        o_ref[0, 0, :width, :] = out.T.astype(BF16)
        o_ref[0, 0, width:, :] = jnp.ones((o_ref.shape[2] - width, n_blk), BF16)
    else:
        o_ref[0, 0] = out.astype(BF16)


def _compress(kvc, pe, w1, w2, bsz, seq, *, width, col_block0, out_rows=None):
    g = NSA_GROUPS
    n_blk = seq // CMP_STRIDE
    n_slabs = width // LANES
    out_block = (1, 1, n_blk, w2.shape[1]) if out_rows is None else (1, 1, out_rows, n_blk)
    slab = lambda j: pl.BlockSpec((seq, LANES), lambda b, gg: (b, (col_block0 + gg) * n_slabs + j))
    return pl.pallas_call(
        functools.partial(_compress_kernel, transposed=out_rows is not None),
        grid=(bsz, g),
        in_specs=[slab(j) for j in range(n_slabs)] + [
                  pl.BlockSpec(pe.shape, lambda b, gg: (0, 0)),
                  pl.BlockSpec(w1.shape, lambda b, gg: (0, 0, 0)),
                  pl.BlockSpec(w2.shape, lambda b, gg: (0, 0))],
        out_specs=pl.BlockSpec(out_block, lambda b, gg: (b, gg, 0, 0)),
        out_shape=jax.ShapeDtypeStruct((bsz, g) + out_block[2:], BF16),
        compiler_params=_params(2),
        name="compress",
    )(*([kvc] * n_slabs), pe, w1, w2)


def _values_t_kernel(v_ref, o_ref):
    o_ref[0, 0, :NSA_DV, :] = v_ref[...].astype(F32).T.astype(BF16)
    o_ref[0, 0, NSA_DV:, :] = jnp.ones((V_ROWS - NSA_DV, v_ref.shape[0]), BF16)


def _values_t(kv, bsz, seq):
    g, dv, dkp = NSA_GROUPS, NSA_DV, NSA_DKP
    branch_blocks = g * (dkp + dv) // dv
    v0 = g * dkp // dv
    return pl.pallas_call(
        _values_t_kernel,
        grid=(bsz, 2 * g),
        in_specs=[pl.BlockSpec((seq, dv), lambda b, j: (b, v0 + (j // g) * branch_blocks + j % g))],
        out_specs=pl.BlockSpec((1, 1, V_ROWS, seq), lambda b, j: (b, j, 0, 0)),
        out_shape=jax.ShapeDtypeStruct((bsz, 2 * g, V_ROWS, seq), BF16),
        compiler_params=_params(2),
        name="values_t",
    )(kv)


def _pad_last(a, width):
    return jnp.pad(a, [(0, 0)] * (a.ndim - 1) + [(0, width - a.shape[-1])])


def _head_pad_matrix(n_heads, width, padded, tail):
    p = np.zeros((n_heads * width + tail, n_heads * padded + tail), np.float32)
    c = np.arange(n_heads * width)
    p[c, (c // width) * padded + c % width] = 1.0
    t = np.arange(tail)
    p[n_heads * width + t, n_heads * padded + t] = 1.0
    return jnp.asarray(p, BF16)


def _gate_regroup_matrix():
    g, r = NSA_GROUPS, NSA_HPG
    p = np.zeros((LANES, g * LANES), np.float32)
    for br in range(N_BRANCH):
        for h in range(NSA_HEADS):
            p[br * NSA_HEADS + h, (h // r) * LANES + br * r + h % r] = 1.0
    return jnp.asarray(p, BF16)


def _nsa_shared_kv(x, w_kv, pe_k, pe_v, w_ck1, w_ck2, w_cv1, w_cv2, bsz, seq):
    g, dk, dv, dkp = NSA_GROUPS, NSA_DK, NSA_DV, NSA_DKP
    place = _head_pad_matrix(g, dk, dkp, g * dv)
    tn_in, tn = place.shape
    kvc = _proj(x, w_kv, n_tiles=1, col_block0=0, tm=512, tn=tn, tn_in=tn_in, out_dtype=F32, place=place)
    kv = _proj(x, w_kv, n_tiles=2, col_block0=1, tm=512, tn=tn, tn_in=tn_in, out_dtype=BF16, place=place)
    w1k = _pad_last(w_ck1.reshape(CMP_LEN, dk, CMP_HID).transpose(0, 2, 1), dkp).transpose(0, 2, 1).astype(BF16)
    w1v = w_cv1.reshape(CMP_LEN, dv, CMP_HID).astype(BF16)
    kc = _compress(kvc, _pad_last(pe_k, dkp), w1k, _pad_last(w_ck2, dkp).astype(BF16), bsz, seq,
                   width=dkp, col_block0=0)
    vct = _compress(kvc, pe_v, w1v, w_cv2.astype(BF16), bsz, seq, width=dv, col_block0=g * dkp // dv,
                    out_rows=V_ROWS)
    return kc, vct, kv, _values_t(kv, bsz, seq)


def _nsa_attn_kernel(q_ref, zc_ref, zs_ref, zw_ref, gate_ref, kc_ref, vct_ref, ovt_ref,
                     ks_ref, vst_ref, kw_ref, vwt_ref, o_ref,
                     bias_ref, sa_ref, sb_ref, m_ref, acc_ref, mix_ref, zsil_ref):
    tq, r, dv, tk = q_ref.shape[0], NSA_HPG, NSA_DV, SEL_TK
    lanes = r * tq
    n_blk = kc_ref.shape[2]
    n_sel = n_blk * CMP_STRIDE // SEL_LEN
    q0 = pl.program_id(2) * tq
    q4 = jnp.concatenate([q_ref[:, h * NSA_DKP:(h + 1) * NSA_DKP] for h in range(r)], axis=0)

    def qpos(shape):
        return q0 + lax.rem(lax.broadcasted_iota(jnp.int32, shape, 1), tq)

    def all_heads(tile):
        return jnp.concatenate([tile] * r, axis=1)

    def weights(s):
        return jnp.exp2(s - jnp.max(s, axis=0, keepdims=True)).astype(BF16)

    gates_t = _sigmoid(gate_ref[...]).T

    def gated(o_aug, inv_l, silu_z, br, h):
        cols = slice(h * tq, (h + 1) * tq)
        row_scale = gates_t[br * r + h:br * r + h + 1, :] * inv_l[:, cols]
        return (o_aug[:dv, cols] * row_scale).T * silu_z

    s = lax.dot_general(kc_ref[0, 0], q4, NT_DIMS, preferred_element_type=F32)
    blk_end = lax.broadcasted_iota(jnp.int32, (n_blk, tq), 0) * CMP_STRIDE + (CMP_LEN - 1)
    s = s + all_heads(jnp.where(blk_end <= qpos((n_blk, tq)), 0.0, NEG_INF))
    eb = weights(s)
    o_cmp = jnp.dot(vct_ref[0, 0], eb, preferred_element_type=F32)
    inv_c = jnp.where(qpos((1, lanes)) >= CMP_LEN - 1, 1.0 / o_cmp[dv:dv + 1, :], 0.0)

    imp = sum(jnp.dot(ovt_ref[...], eb[:, h * tq:(h + 1) * tq], preferred_element_type=F32)
              * inv_c[:, h * tq:(h + 1) * tq] for h in range(r))
    imp = imp[:n_sel, :]
    blk = lax.broadcasted_iota(jnp.int32, imp.shape, 0)
    cur = (q0 + lax.broadcasted_iota(jnp.int32, imp.shape, 1)) // SEL_LEN
    forced = (blk == 0) | (blk == cur) | (blk == cur - 1)
    score = jnp.where(forced, FORCE_SCORE, jnp.where(blk <= cur, imp, -1.0))
    groups = [score[g0:g0 + SUBLANES, :] for g0 in range(0, n_sel, SUBLANES)]
    ranks = [jnp.zeros((SUBLANES, tq), F32) for _ in groups]
    for i in range(n_sel):
        row = score[i:i + 1, :]
        for gi, grp in enumerate(groups):
            g0 = gi * SUBLANES
            if g0 > i:
                ahead = row >= grp
            elif g0 + SUBLANES - 1 < i:
                ahead = row > grp
            else:
                after = lax.broadcasted_iota(jnp.int32, grp.shape, 0) + g0 > i
                ahead = (row > grp) | (after & (row == grp))
            ranks[gi] = ranks[gi] + jnp.where(ahead, 1.0, 0.0)
    rank = jnp.concatenate(ranks, axis=0)
    bias_ref[...] = jnp.where(rank < float(min(SEL_TOPK, n_sel)), 0.0, NEG_INF)

    span = WINDOW + tq
    k0 = pl.multiple_of(jnp.maximum(q0 - WINDOW, 0), tq)
    s = lax.dot_general(kw_ref[pl.ds(k0, span), :], q4, NT_DIMS, preferred_element_type=F32)
    dist = qpos((span, tq)) - (k0 + lax.broadcasted_iota(jnp.int32, (span, tq), 0))
    s = s + all_heads(jnp.where((dist >= 0) & (dist < WINDOW), 0.0, NEG_INF))
    o_win = jnp.dot(vwt_ref[0, 0, :, pl.ds(k0, span)], weights(s), preferred_element_type=F32)
    inv_w = 1.0 / o_win[dv:dv + 1, :]

    for h in range(r):
        cols = slice(h * dv, (h + 1) * dv)
        mix_ref[:, cols] = (gated(o_cmp, inv_c, _silu(zc_ref[:, cols].astype(F32)), 0, h)
                            + gated(o_win, inv_w, _silu(zw_ref[:, cols].astype(F32)), 2, h))
    zsil_ref[...] = _silu(zs_ref[...].astype(F32))

    m_ref[...] = jnp.full(m_ref.shape, NEG_INF, F32)
    acc_ref[...] = jnp.zeros_like(acc_ref)
    blocks_per_tile = tk // SEL_LEN
    n_tiles = q0 // tk + 1
    causal_gap = (q0 + lax.broadcasted_iota(jnp.int32, (tk, tq), 1)) - lax.broadcasted_iota(jnp.int32, (tk, tq), 0)

    def scores(tile):
        kbase = pl.multiple_of(tile * tk, tk)
        rows = bias_ref[pl.ds(pl.multiple_of(tile * blocks_per_tile, blocks_per_tile), blocks_per_tile), :]
        bias = jnp.concatenate([jnp.broadcast_to(rows[j:j + 1, :], (SEL_LEN, tq)) for j in range(blocks_per_tile)],
                               axis=0)
        bias = jnp.where(causal_gap >= kbase, bias, NEG_INF)
        return (lax.dot_general(ks_ref[pl.ds(kbase, tk), :], q4, NT_DIMS, preferred_element_type=F32)
                + all_heads(bias))

    def attend(s_ref, tile):
        kbase = pl.multiple_of(tile * tk, tk)
        s = s_ref[...]
        m_prev = m_ref[...]
        m_new = jnp.maximum(m_prev, jnp.max(s, axis=0, keepdims=True))
        pv = jnp.dot(vst_ref[0, 0, :, pl.ds(kbase, tk)], jnp.exp2(s - m_new).astype(BF16),
                     preferred_element_type=F32)
        acc_ref[...] = jnp.exp2(m_prev - m_new) * acc_ref[...] + pv
        m_ref[...] = m_new

    sa_ref[...] = scores(0)

    def pair(jp, carry):
        first = 2 * jp
        sb_ref[...] = scores(first + 1)
        attend(sa_ref, first)
        sa_ref[...] = scores(jnp.minimum(first + 2, n_tiles - 1))
        attend(sb_ref, first + 1)
        return carry

    lax.fori_loop(0, n_tiles // 2, pair, 0)

    @pl.when(lax.rem(n_tiles, 2) == 1)
    def _():
        attend(sa_ref, n_tiles - 1)

    o_sel = acc_ref[...]
    inv_s = 1.0 / o_sel[dv:dv + 1, :]
    for h in range(r):
        cols = slice(h * dv, (h + 1) * dv)
        o_ref[:, cols] = (mix_ref[:, cols] + gated(o_sel, inv_s, zsil_ref[:, cols], 1, h)).astype(BF16)


def _overlap_matrix_t(n_blk, n_sel):
    a, b = SEL_LEN // CMP_STRIDE, CMP_LEN // CMP_STRIDE
    span = a + b - 2
    n = np.arange(n_blk)[None, :]
    j = np.arange(LANES)[:, None]
    diff = n - a * j
    ov = np.where((diff >= 0) & (diff <= span), np.minimum(np.minimum(diff, span - diff), min(a, b) - 1) + 1, 0)
    ov = np.where((n < n_blk - 1) & (j < n_sel), ov, 0)
    return jnp.asarray(ov, BF16)


def _nsa_attention(qp, zp, gate, kc, vct, kv, vt, bsz, seq):
    g, r, dv, dkp, tq, tk = NSA_GROUPS, NSA_HPG, NSA_DV, NSA_DKP, ATT_TQ, SEL_TK
    nq = seq // tq
    n_blk = seq // CMP_STRIDE
    n_sel = seq // SEL_LEN
    ovt = _overlap_matrix_t(n_blk, n_sel)
    row = lambda b, gg, i: b * nq + i
    z_spec = lambda br: pl.BlockSpec((tq, r * dv), lambda b, gg, i: (row(b, gg, i), br * g + gg))
    ks0, kw0 = 0, g * (dkp + dv) // dkp
    return pl.pallas_call(
        _nsa_attn_kernel,
        grid=(bsz, g, nq),
        in_specs=[pl.BlockSpec((tq, r * dkp), lambda b, gg, i: (row(b, gg, i), gg)),
                  z_spec(0), z_spec(1), z_spec(2),
                  pl.BlockSpec((tq, LANES), lambda b, gg, i: (row(b, gg, i), gg)),
                  pl.BlockSpec((1, 1, n_blk, dkp), lambda b, gg, i: (b, gg, 0, 0)),
                  pl.BlockSpec((1, 1, V_ROWS, n_blk), lambda b, gg, i: (b, gg, 0, 0)),
                  pl.BlockSpec(ovt.shape, lambda b, gg, i: (0, 0)),
                  pl.BlockSpec((seq, dkp), lambda b, gg, i: (b, ks0 + gg)),
                  pl.BlockSpec((1, 1, V_ROWS, seq), lambda b, gg, i: (b, gg, 0, 0)),
                  pl.BlockSpec((seq, dkp), lambda b, gg, i: (b, kw0 + gg)),
                  pl.BlockSpec((1, 1, V_ROWS, seq), lambda b, gg, i: (b, g + gg, 0, 0))],
        out_specs=pl.BlockSpec((tq, r * dv), lambda b, gg, i: (row(b, gg, i), gg)),
        out_shape=jax.ShapeDtypeStruct((bsz * seq, g * r * dv), BF16),
        scratch_shapes=[pltpu.VMEM((n_sel, tq), F32),
                        pltpu.VMEM((tk, r * tq), F32),
                        pltpu.VMEM((tk, r * tq), F32),
                        pltpu.VMEM((1, r * tq), F32),
                        pltpu.VMEM((V_ROWS, r * tq), F32),
                        pltpu.VMEM((tq, r * dv), F32),
                        pltpu.VMEM((tq, r * dv), F32)],
        compiler_params=_params(3),
        name="nsa_attn",
    )(qp, zp, zp, zp, gate, kc, vct, ovt, kv, vt, kv, vt)


def _nsa_layer(x, shared, w_in, w_out, gain, bias, bsz, seq):
    kc, vct, kv, vt = shared
    h, g, r, dk, dv, dkp = NSA_HEADS, NSA_GROUPS, NSA_HPG, NSA_DK, NSA_DV, NSA_DKP
    nq = h * dk
    nz = N_BRANCH * h * dv
    tn = 1024
    w_t = w_in.T
    qp = _proj(x, w_t, n_tiles=g, col_block0=0, tm=1024, tn=r * dkp, tn_in=r * dk, out_dtype=BF16, w_transposed=True,
               place=_head_pad_matrix(r, dk, dkp, 0).T, out_scale=(dk ** -0.5) * math.log2(math.e))
    zp = _proj(x, w_t, n_tiles=nz // tn, col_block0=nq // tn, tm=1024, tn=tn, out_dtype=BF16, w_transposed=True)
    gate = _proj(x, w_t, n_tiles=1, col_block0=(nq + nz) // LANES, tm=1024, tn=g * LANES, tn_in=LANES,
                 out_dtype=F32, w_transposed=True, place=_gate_regroup_matrix().T, valid_cols=N_BRANCH * h)
    mixed = _nsa_attention(qp, zp, gate, kc, vct, kv, vt, bsz, seq)
    return _out_ln(mixed, w_out.astype(BF16), x, gain, bias)


def kernel(x, positions, ret_w_in_0, ret_w_out_0, ln_g_0, ln_b_0, ret_w_in_1, ret_w_out_1, ln_g_1, ln_b_1,
           nsa_w_kv, nsa_pe_k, nsa_pe_v, nsa_w_ck1, nsa_w_ck2, nsa_w_cv1, nsa_w_cv2,
           nsa_w_in_2, nsa_w_out_2, ln_g_2, ln_b_2, nsa_w_in_3, nsa_w_out_3, ln_g_3, ln_b_3):
    bsz, seq, d = x.shape
    h = x.reshape(bsz * seq, d)
    cos, sin = _rope_tables(positions)
    h = _retention_layer(h, cos, sin, ret_w_in_0, ret_w_out_0, ln_g_0, ln_b_0, bsz, seq)
    h = _retention_layer(h, cos, sin, ret_w_in_1, ret_w_out_1, ln_g_1, ln_b_1, bsz, seq)
    shared = _nsa_shared_kv(h, nsa_w_kv, nsa_pe_k, nsa_pe_v, nsa_w_ck1, nsa_w_ck2, nsa_w_cv1, nsa_w_cv2, bsz, seq)
    h = _nsa_layer(h, shared, nsa_w_in_2, nsa_w_out_2, ln_g_2, ln_b_2, bsz, seq)
    h = _nsa_layer(h, shared, nsa_w_in_3, nsa_w_out_3, ln_g_3, ln_b_3, bsz, seq)
    return h.reshape(bsz, seq, d)
```

```python
import functools
import math

import numpy as np
import jax
import jax.numpy as jnp
from jax import lax
from jax.experimental import pallas as pl
from jax.experimental.pallas import tpu as pltpu

D_MODEL = 2048
DEPTH = 4
N_A_LAYERS = DEPTH // 2
RET_HEADS = 8
RET_DK = D_MODEL // RET_HEADS
RET_DV = 2 * D_MODEL // RET_HEADS
RET_CHUNK = 128
RET_THETA_BASE = 10000.0
NSA_HEADS = 16
NSA_GROUPS = 4
NSA_HPG = NSA_HEADS // NSA_GROUPS
NSA_DV = D_MODEL // NSA_HEADS
NSA_DK = 3 * NSA_DV // 2
NSA_DKP = 256
CMP_LEN = 32
CMP_STRIDE = 16
CMP_HID = 2 * NSA_DV
SEL_LEN = 64
SEL_TOPK = 16
WINDOW = 512
N_BRANCH = 3
ALPHA = (2.0 * DEPTH) ** 0.25
NEG_INF = -1e30
FORCE_SCORE = 1e9
LN_EPS = 1e-5

LANES = 128
SUBLANES = 8
BF16_SUBLANES = 16
ATT_TQ = 256
SEL_TK = 512
V_ROWS = NSA_DV + BF16_SUBLANES
VMEM_LIMIT = 56 * 1024 * 1024

F32 = jnp.float32
BF16 = jnp.bfloat16
NT_DIMS = (((1,), (1,)), ((), ()))
TN_DIMS = (((0,), (0,)), ((), ()))


def _params(n_axes):
    return pltpu.CompilerParams(dimension_semantics=("arbitrary",) * n_axes, vmem_limit_bytes=VMEM_LIMIT)


def _sigmoid(x):
    return 0.5 * jnp.tanh(0.5 * x) + 0.5


def _silu(x):
    half = 0.5 * x
    return half + half * jnp.tanh(half)


def _proj_kernel(x_ref, w_ref, *rest, out_scale, w_transposed):
    place_ref = rest[0] if len(rest) == 3 else None
    o_ref, wbf_ref = rest[-2:]

    @pl.when(pl.program_id(1) == 0)
    def _():
        if place_ref is None:
            wbf_ref[...] = w_ref[...].astype(BF16)
        else:
            b_in, b_out = place_ref.shape[::-1] if w_transposed else place_ref.shape
            for c in range(w_ref.shape[0 if w_transposed else 1] // b_in):
                src, dst = slice(c * b_in, (c + 1) * b_in), slice(c * b_out, (c + 1) * b_out)
                if w_transposed:
                    wbf_ref[dst, :] = jnp.dot(place_ref[...], w_ref[src, :].astype(BF16),
                                              preferred_element_type=F32).astype(BF16)
                else:
                    wbf_ref[:, dst] = jnp.dot(w_ref[:, src].astype(BF16), place_ref[...],
                                              preferred_element_type=F32).astype(BF16)

    xb = x_ref[...]
    if xb.dtype != BF16:
        xb = xb.astype(BF16)
    if w_transposed:
        acc = lax.dot_general(xb, wbf_ref[...], NT_DIMS, preferred_element_type=F32)
    else:
        acc = jnp.dot(xb, wbf_ref[...], preferred_element_type=F32)
    o_ref[...] = (acc if out_scale is None else acc * out_scale).astype(o_ref.dtype)


def _proj(x, w, *, n_tiles, col_block0, tm, tn, out_dtype, tn_in=None, place=None, out_scale=None,
          w_transposed=False):
    m, k = x.shape
    tn_in = tn if tn_in is None else tn_in
    if w_transposed:
        w_spec = pl.BlockSpec((tn_in, k), lambda j, i: (j + col_block0, 0))
    else:
        w_spec = pl.BlockSpec((k, tn_in), lambda j, i: (0, j + col_block0))
    in_specs = [pl.BlockSpec((tm, k), lambda j, i: (i, 0)), w_spec]
    args = [x, w]
    if place is not None:
        in_specs.append(pl.BlockSpec(place.shape, lambda j, i: (0, 0)))
        args.append(place)
    return pl.pallas_call(
        functools.partial(_proj_kernel, out_scale=out_scale, w_transposed=w_transposed),
        grid=(n_tiles, m // tm),
        in_specs=in_specs,
        out_specs=pl.BlockSpec((tm, tn), lambda j, i: (i, j)),
        out_shape=jax.ShapeDtypeStruct((m, n_tiles * tn), out_dtype),
        scratch_shapes=[pltpu.VMEM((tn, k) if w_transposed else (k, tn), BF16)],
        compiler_params=_params(2),
        name="proj",
    )(*args)


def _out_ln_kernel(a_ref, w_ref, x_ref, g_ref, b_ref, o_ref, *o16_ref):
    u = ALPHA * x_ref[...] + jnp.dot(a_ref[...], w_ref[...], preferred_element_type=F32)
    mu = jnp.mean(u, axis=-1, keepdims=True)
    var = jnp.mean(jnp.square(u - mu), axis=-1, keepdims=True)
    y = (u - mu) * lax.rsqrt(var + LN_EPS) * g_ref[...] + b_ref[...]
    o_ref[...] = y
    if o16_ref:
        o16_ref[0][...] = y.astype(BF16)


def _out_ln(a, w, x, gain, bias, *, tm=512, with_bf16=True):
    m, ka = a.shape
    d = w.shape[1]
    o_spec = pl.BlockSpec((tm, d), lambda i: (i, 0))
    return pl.pallas_call(
        _out_ln_kernel,
        grid=(m // tm,),
        in_specs=[pl.BlockSpec((tm, ka), lambda i: (i, 0)),
                  pl.BlockSpec((ka, d), lambda i: (0, 0), pipeline_mode=pl.Buffered(1)),
                  pl.BlockSpec((tm, d), lambda i: (i, 0)),
                  pl.BlockSpec((1, d), lambda i: (0, 0)),
                  pl.BlockSpec((1, d), lambda i: (0, 0))],
        out_specs=[o_spec, o_spec] if with_bf16 else o_spec,
        out_shape=([jax.ShapeDtypeStruct((m, d), F32), jax.ShapeDtypeStruct((m, d), BF16)] if with_bf16
                   else jax.ShapeDtypeStruct((m, d), F32)),
        compiler_params=_params(1),
        name="out_ln",
    )(a, w, x, gain.reshape(1, d), bias.reshape(1, d))


def _rope_kernel(pos_ref, freq_ref, cos_ref, sin_ref):
    ang = pos_ref[...].astype(F32) * freq_ref[...]
    cos_ref[...] = jnp.cos(ang)
    sin_ref[...] = jnp.sin(ang)


def _rope_tables(positions, tm=1024):
    t = positions.size
    half = RET_DK // 2
    inv_freq = 1.0 / (RET_THETA_BASE ** jnp.linspace(0.0, 1.0, half, dtype=F32))
    return pl.pallas_call(
        _rope_kernel,
        grid=(t // tm,),
        in_specs=[pl.BlockSpec((tm, 1), lambda i: (i, 0)), pl.BlockSpec((1, half), lambda i: (0, 0))],
        out_specs=[pl.BlockSpec((tm, half), lambda i: (i, 0))] * 2,
        out_shape=[jax.ShapeDtypeStruct((t, half), F32)] * 2,
        compiler_params=_params(1),
        name="rope_tables",
    )(positions.reshape(t, 1), inv_freq.reshape(1, half))


def _retention_kernel(q_ref, k_ref, v_ref, z_ref, cos_ref, sin_ref, dintra_ref, dq_ref, dk_ref, dch_ref,
                      o_ref, state_ref, *, n_sub, n_heads):
    c = RET_CHUNK
    half = RET_DK // 2

    @pl.when(pl.program_id(2) == 0)
    def _():
        state_ref[...] = jnp.zeros_like(state_ref)

    def rotate(t, cosv, sinv):
        even, odd = t[:, :half], t[:, half:]
        return jnp.concatenate([even * cosv - odd * sinv, even * sinv + odd * cosv], axis=1)

    for j in range(n_sub):
        rows = pl.ds(j * c, c)
        cosv, sinv = cos_ref[rows, :], sin_ref[rows, :]
        for hh in range(n_heads):
            kcols = slice(hh * RET_DK, (hh + 1) * RET_DK)
            vcols = slice(hh * RET_DV, (hh + 1) * RET_DV)
            qr = rotate(q_ref[rows, kcols].astype(F32), cosv, sinv)
            kr = rotate(k_ref[rows, kcols].astype(F32), cosv, sinv)
            qb, kb = qr.astype(BF16), kr.astype(BF16)
            kdb = (kr * dk_ref[hh]).astype(BF16)
            v = v_ref[rows, vcols]
            state = state_ref[hh]
            scores = lax.dot_general(qb, kb, NT_DIMS, preferred_element_type=F32) * dintra_ref[hh]
            out = (jnp.dot(scores.astype(BF16), v, preferred_element_type=F32)
                   + jnp.dot(qb, state.astype(BF16), preferred_element_type=F32) * dq_ref[hh])
            state_ref[hh] = state * dch_ref[hh] + lax.dot_general(kdb, v, TN_DIMS, preferred_element_type=F32)
            mu = jnp.mean(out, axis=-1, keepdims=True)
            var = jnp.mean(jnp.square(out - mu), axis=-1, keepdims=True)
            normed = (out - mu) * lax.rsqrt(var + LN_EPS)
            o_ref[rows, vcols] = (normed * _silu(z_ref[rows, vcols].astype(F32))).astype(BF16)


def _retention_decays():
    h, c = RET_HEADS, RET_CHUNK
    k_scale = RET_DK ** -0.5
    assert math.frexp(k_scale)[0] == 0.5
    log_gamma = jnp.log1p(-jnp.exp2(-5.0 - jnp.arange(h, dtype=F32)))
    idx = jnp.arange(c, dtype=F32)
    rel = idx[:, None] - idx[None, :]
    dintra = jnp.where(rel >= 0, jnp.exp(log_gamma[:, None, None] * jnp.maximum(rel, 0.0)), 0.0) * k_scale
    dq = jnp.exp(log_gamma[:, None] * (idx + 1.0))
    dk = jnp.exp(log_gamma[:, None] * (c - 1.0 - idx)) * k_scale
    dch = jnp.exp(log_gamma * c)
    return (dintra,
            jnp.broadcast_to(dq[:, :, None], (h, c, RET_DV)),
            jnp.broadcast_to(dk[:, :, None], (h, c, RET_DK)),
            jnp.broadcast_to(dch[:, None, None], (h, 1, RET_DV)))


def _retention(qk, vz, cos, sin, bsz, seq, *, ts=1024, hps=4):
    h, dk, dv, c = RET_HEADS, RET_DK, RET_DV, RET_CHUNK
    ns = seq // ts
    ng = h // hps
    dintra, dq, dkt, dch = _retention_decays()
    row = lambda b, hh, s: b * ns + s
    return pl.pallas_call(
        functools.partial(_retention_kernel, n_sub=ts // c, n_heads=hps),
        grid=(bsz, ng, ns),
        in_specs=[pl.BlockSpec((ts, hps * dk), lambda b, hh, s: (row(b, hh, s), hh)),
                  pl.BlockSpec((ts, hps * dk), lambda b, hh, s: (row(b, hh, s), ng + hh)),
                  pl.BlockSpec((ts, hps * dv), lambda b, hh, s: (row(b, hh, s), hh)),
                  pl.BlockSpec((ts, hps * dv), lambda b, hh, s: (row(b, hh, s), ng + hh)),
                  pl.BlockSpec((ts, dk // 2), lambda b, hh, s: (row(b, hh, s), 0)),
                  pl.BlockSpec((ts, dk // 2), lambda b, hh, s: (row(b, hh, s), 0)),
                  pl.BlockSpec((hps, c, c), lambda b, hh, s: (hh, 0, 0)),
                  pl.BlockSpec((hps, c, dv), lambda b, hh, s: (hh, 0, 0)),
                  pl.BlockSpec((hps, c, dk), lambda b, hh, s: (hh, 0, 0)),
                  pl.BlockSpec((hps, 1, dv), lambda b, hh, s: (hh, 0, 0))],
        out_specs=pl.BlockSpec((ts, hps * dv), lambda b, hh, s: (row(b, hh, s), hh)),
        out_shape=jax.ShapeDtypeStruct((bsz * seq, h * dv), BF16),
        scratch_shapes=[pltpu.VMEM((hps, dk, dv), F32)],
        compiler_params=_params(3),
        name="retention",
    )(qk, qk, vz, vz, cos, sin, dintra, dq, dkt, dch)


def _deinterleave_matrix(width):
    p = np.zeros((width, width), np.float32)
    j = np.arange(width // 2)
    p[2 * j, j] = 1.0
    p[2 * j + 1, width // 2 + j] = 1.0
    return jnp.asarray(p, BF16)


def _proj_rows(xm):
    return 2048 if xm.dtype == BF16 else 1024


def _retention_layer(x, xm, cos, sin, w_in, w_out, gain, bias, bsz, seq):
    h, dk, dv = RET_HEADS, RET_DK, RET_DV
    tn, tm = 1024, _proj_rows(xm)
    qk = _proj(xm, w_in, n_tiles=2 * h * dk // tn, col_block0=0, tm=tm, tn=tn, out_dtype=BF16,
               place=_deinterleave_matrix(dk))
    vz = _proj(xm, w_in, n_tiles=2 * h * dv // tn, col_block0=2 * h * dk // tn, tm=tm, tn=tn, out_dtype=BF16)
    mixed = _retention(qk, vz, cos, sin, bsz, seq)
    return _out_ln(mixed, w_out.astype(BF16), x, gain, bias)


def _compress_kernel(*refs, transposed):
    x_refs, (pe_ref, w1_ref, w2_ref, o_ref) = refs[:-4], refs[-4:]
    n_blk = o_ref.shape[3] if transposed else o_ref.shape[2]
    half = CMP_LEN // 2
    acc_lo = jnp.zeros((n_blk, CMP_HID), F32)
    acc_hi = jnp.zeros((n_blk, CMP_HID), F32)
    for l in range(half):
        rows = jnp.concatenate([x_ref[pl.ds(l, n_blk, stride=CMP_STRIDE), :] for x_ref in x_refs], axis=1)
        acc_lo += jnp.dot((rows + pe_ref[l:l + 1, :]).astype(BF16), w1_ref[l], preferred_element_type=F32)
        acc_hi += jnp.dot((rows + pe_ref[half + l:half + l + 1, :]).astype(BF16), w1_ref[half + l],
                          preferred_element_type=F32)
    hid = acc_lo + pltpu.roll(acc_hi, shift=n_blk - 1, axis=0)
    out = jnp.dot(_silu(hid).astype(BF16), w2_ref[...], preferred_element_type=F32)
    keep = lax.broadcasted_iota(jnp.int32, out.shape, 0) < n_blk - 1
    out = jnp.where(keep, out, 0.0)
    if transposed:
        width = out.shape[1]
        o_ref[0, 0, :width, :] = out.T.astype(BF16)
        o_ref[0, 0, width:, :] = jnp.ones((o_ref.shape[2] - width, n_blk), BF16)
    else:
        o_ref[0, 0] = out.astype(BF16)


def _compress(kvc, pe, w1, w2, bsz, seq, *, width, col_block0, out_rows=None):
    g = NSA_GROUPS
    n_blk = seq // CMP_STRIDE
    n_slabs = width // LANES
    out_block = (1, 1, n_blk, w2.shape[1]) if out_rows is None else (1, 1, out_rows, n_blk)
    slab = lambda j: pl.BlockSpec((seq, LANES), lambda b, gg: (b, (col_block0 + gg) * n_slabs + j))
    return pl.pallas_call(
        functools.partial(_compress_kernel, transposed=out_rows is not None),
        grid=(bsz, g),
        in_specs=[slab(j) for j in range(n_slabs)] + [
                  pl.BlockSpec(pe.shape, lambda b, gg: (0, 0)),
                  pl.BlockSpec(w1.shape, lambda b, gg: (0, 0, 0)),
                  pl.BlockSpec(w2.shape, lambda b, gg: (0, 0))],
        out_specs=pl.BlockSpec(out_block, lambda b, gg: (b, gg, 0, 0)),
        out_shape=jax.ShapeDtypeStruct((bsz, g) + out_block[2:], BF16),
        compiler_params=_params(2),
        name="compress",
    )(*([kvc] * n_slabs), pe, w1, w2)


def _values_t_kernel(v_ref, o_ref):
    o_ref[0, 0, :NSA_DV, :] = v_ref[...].astype(F32).T.astype(BF16)
    o_ref[0, 0, NSA_DV:, :] = jnp.ones((V_ROWS - NSA_DV, v_ref.shape[0]), BF16)


def _values_t(kv, bsz, seq):
    g, dv, dkp = NSA_GROUPS, NSA_DV, NSA_DKP
    branch_blocks = g * (dkp + dv) // dv
    v0 = g * dkp // dv
    return pl.pallas_call(
        _values_t_kernel,
        grid=(bsz, 2 * g),
        in_specs=[pl.BlockSpec((seq, dv), lambda b, j: (b, v0 + (j // g) * branch_blocks + j % g))],
        out_specs=pl.BlockSpec((1, 1, V_ROWS, seq), lambda b, j: (b, j, 0, 0)),
        out_shape=jax.ShapeDtypeStruct((bsz, 2 * g, V_ROWS, seq), BF16),
        compiler_params=_params(2),
        name="values_t",
    )(kv)


def _pad_last(a, width):
    return jnp.pad(a, [(0, 0)] * (a.ndim - 1) + [(0, width - a.shape[-1])])


def _head_pad_matrix(n_heads, width, padded, tail):
    p = np.zeros((n_heads * width + tail, n_heads * padded + tail), np.float32)
    c = np.arange(n_heads * width)
    p[c, (c // width) * padded + c % width] = 1.0
    t = np.arange(tail)
    p[n_heads * width + t, n_heads * padded + t] = 1.0
    return jnp.asarray(p, BF16)


def _nsa_shared_kv(x, w_kv, pe_k, pe_v, w_ck1, w_ck2, w_cv1, w_cv2, bsz, seq):
    g, dk, dv, dkp = NSA_GROUPS, NSA_DK, NSA_DV, NSA_DKP
    place = _head_pad_matrix(g, dk, dkp, g * dv)
    tn_in, tn = place.shape
    tm = 512
    kvc = _proj(x, w_kv, n_tiles=1, col_block0=0, tm=tm, tn=tn, tn_in=tn_in, out_dtype=F32, place=place)
    kv = _proj(x, w_kv, n_tiles=2, col_block0=1, tm=tm, tn=tn, tn_in=tn_in, out_dtype=BF16, place=place)
    w1k = _pad_last(w_ck1.reshape(CMP_LEN, dk, CMP_HID).transpose(0, 2, 1), dkp).transpose(0, 2, 1).astype(BF16)
    w1v = w_cv1.reshape(CMP_LEN, dv, CMP_HID).astype(BF16)
    kc = _compress(kvc, _pad_last(pe_k, dkp), w1k, _pad_last(w_ck2, dkp).astype(BF16), bsz, seq,
                   width=dkp, col_block0=0)
    vct = _compress(kvc, pe_v, w1v, w_cv2.astype(BF16), bsz, seq, width=dv, col_block0=g * dkp // dv,
                    out_rows=V_ROWS)
    return kc, vct, kv, _values_t(kv, bsz, seq)


def _nsa_attn_kernel(q_ref, zc_ref, zs_ref, zw_ref, x_ref, wg_ref, kc_ref, vct_ref, ovt_ref,
                     ks_ref, vst_ref, kw_ref, vwt_ref, o_ref,
                     bias_ref, sa_ref, sb_ref, m_ref, acc_ref, mix_ref, zsil_ref, wgb_ref, gates_ref):
    tq, r, dv, tk = q_ref.shape[0], NSA_HPG, NSA_DV, SEL_TK
    lanes = r * tq
    n_blk = kc_ref.shape[2]
    n_sel = n_blk * CMP_STRIDE // SEL_LEN
    q0 = pl.program_id(2) * tq
    q4 = jnp.concatenate([q_ref[:, h * NSA_DKP:(h + 1) * NSA_DKP] for h in range(r)], axis=0)

    def qpos(shape):
        return q0 + lax.rem(lax.broadcasted_iota(jnp.int32, shape, 1), tq)

    def all_heads(tile):
        return jnp.concatenate([tile] * r, axis=1)

    def weights(s):
        return jnp.exp2(s - jnp.max(s, axis=0, keepdims=True)).astype(BF16)

    @pl.when((pl.program_id(0) == 0) & (pl.program_id(1) == 0) & (pl.program_id(2) == 0))
    def _():
        real = lax.broadcasted_iota(jnp.int32, wg_ref.shape, 0) < N_BRANCH * NSA_HEADS
        wgb_ref[...] = jnp.where(real, wg_ref[...], 0.0).astype(BF16)

    gates_ref[...] = _sigmoid(lax.dot_general(wgb_ref[...], x_ref[...], NT_DIMS, preferred_element_type=F32))
    head0 = pl.program_id(1) * r

    def gated(o_aug, inv_l, silu_z, br, h):
        cols = slice(h * tq, (h + 1) * tq)
        row_scale = gates_ref[pl.ds(br * NSA_HEADS + head0 + h, 1), :] * inv_l[:, cols]
        return (o_aug[:dv, cols] * row_scale).T * silu_z

    s = lax.dot_general(kc_ref[0, 0], q4, NT_DIMS, preferred_element_type=F32)
    blk_end = lax.broadcasted_iota(jnp.int32, (n_blk, tq), 0) * CMP_STRIDE + (CMP_LEN - 1)
    s = s + all_heads(jnp.where(blk_end <= qpos((n_blk, tq)), 0.0, NEG_INF))
    eb = weights(s)
    o_cmp = jnp.dot(vct_ref[0, 0], eb, preferred_element_type=F32)
    inv_c = jnp.where(qpos((1, lanes)) >= CMP_LEN - 1, 1.0 / o_cmp[dv:dv + 1, :], 0.0)

    imp = sum(jnp.dot(ovt_ref[...], eb[:, h * tq:(h + 1) * tq], preferred_element_type=F32)
              * inv_c[:, h * tq:(h + 1) * tq] for h in range(r))
    imp = imp[:n_sel, :]
    blk = lax.broadcasted_iota(jnp.int32, imp.shape, 0)
    cur = (q0 + lax.broadcasted_iota(jnp.int32, imp.shape, 1)) // SEL_LEN
    forced = (blk == 0) | (blk == cur) | (blk == cur - 1)
    score = jnp.where(forced, FORCE_SCORE, jnp.where(blk <= cur, imp, -1.0))
    groups = [score[g0:g0 + SUBLANES, :] for g0 in range(0, n_sel, SUBLANES)]
    ranks = [jnp.zeros((SUBLANES, tq), F32) for _ in groups]
    for i in range(n_sel):
        row = score[i:i + 1, :]
        for gi, grp in enumerate(groups):
            g0 = gi * SUBLANES
            if g0 > i:
                ahead = row >= grp
            elif g0 + SUBLANES - 1 < i:
                ahead = row > grp
            else:
                after = lax.broadcasted_iota(jnp.int32, grp.shape, 0) + g0 > i
                ahead = (row > grp) | (after & (row == grp))
            ranks[gi] = ranks[gi] + jnp.where(ahead, 1.0, 0.0)
    rank = jnp.concatenate(ranks, axis=0)
    bias_ref[...] = jnp.where(rank < float(min(SEL_TOPK, n_sel)), 0.0, NEG_INF)

    span = WINDOW + tq
    k0 = pl.multiple_of(jnp.maximum(q0 - WINDOW, 0), tq)
    s = lax.dot_general(kw_ref[pl.ds(k0, span), :], q4, NT_DIMS, preferred_element_type=F32)
    dist = qpos((span, tq)) - (k0 + lax.broadcasted_iota(jnp.int32, (span, tq), 0))
    s = s + all_heads(jnp.where((dist >= 0) & (dist < WINDOW), 0.0, NEG_INF))
    o_win = jnp.dot(vwt_ref[0, 0, :, pl.ds(k0, span)], weights(s), preferred_element_type=F32)
    inv_w = 1.0 / o_win[dv:dv + 1, :]

    for h in range(r):
        cols = slice(h * dv, (h + 1) * dv)
        mix_ref[:, cols] = (gated(o_cmp, inv_c, _silu(zc_ref[:, cols].astype(F32)), 0, h)
                            + gated(o_win, inv_w, _silu(zw_ref[:, cols].astype(F32)), 2, h))
    zsil_ref[...] = _silu(zs_ref[...].astype(F32))

    m_ref[...] = jnp.full(m_ref.shape, NEG_INF, F32)
    acc_ref[...] = jnp.zeros_like(acc_ref)
    blocks_per_tile = tk // SEL_LEN
    n_tiles = q0 // tk + 1
    causal_gap = (q0 + lax.broadcasted_iota(jnp.int32, (tk, tq), 1)) - lax.broadcasted_iota(jnp.int32, (tk, tq), 0)

    def scores(tile):
        kbase = pl.multiple_of(tile * tk, tk)
        rows = bias_ref[pl.ds(pl.multiple_of(tile * blocks_per_tile, blocks_per_tile), blocks_per_tile), :]
        bias = jnp.concatenate([jnp.broadcast_to(rows[j:j + 1, :], (SEL_LEN, tq)) for j in range(blocks_per_tile)],
                               axis=0)
        bias = jnp.where(causal_gap >= kbase, bias, NEG_INF)
        return (lax.dot_general(ks_ref[pl.ds(kbase, tk), :], q4, NT_DIMS, preferred_element_type=F32)
                + all_heads(bias))

    def attend(s_ref, tile):
        kbase = pl.multiple_of(tile * tk, tk)
        s = s_ref[...]
        m_prev = m_ref[...]
        m_new = jnp.maximum(m_prev, jnp.max(s, axis=0, keepdims=True))
        pv = jnp.dot(vst_ref[0, 0, :, pl.ds(kbase, tk)], jnp.exp2(s - m_new).astype(BF16),
                     preferred_element_type=F32)
        acc_ref[...] = jnp.exp2(m_prev - m_new) * acc_ref[...] + pv
        m_ref[...] = m_new

    sa_ref[...] = scores(0)

    def pair(jp, carry):
        first = 2 * jp
        sb_ref[...] = scores(first + 1)
        attend(sa_ref, first)
        sa_ref[...] = scores(jnp.minimum(first + 2, n_tiles - 1))
        attend(sb_ref, first + 1)
        return carry

    lax.fori_loop(0, n_tiles // 2, pair, 0)

    @pl.when(lax.rem(n_tiles, 2) == 1)
    def _():
        attend(sa_ref, n_tiles - 1)

    o_sel = acc_ref[...]
    inv_s = 1.0 / o_sel[dv:dv + 1, :]
    for h in range(r):
        cols = slice(h * dv, (h + 1) * dv)
        o_ref[:, cols] = (mix_ref[:, cols] + gated(o_sel, inv_s, zsil_ref[:, cols], 1, h)).astype(BF16)


def _overlap_matrix_t(n_blk, n_sel):
    a, b = SEL_LEN // CMP_STRIDE, CMP_LEN // CMP_STRIDE
    span = a + b - 2
    n = np.arange(n_blk)[None, :]
    j = np.arange(LANES)[:, None]
    diff = n - a * j
    ov = np.where((diff >= 0) & (diff <= span), np.minimum(np.minimum(diff, span - diff), min(a, b) - 1) + 1, 0)
    ov = np.where((n < n_blk - 1) & (j < n_sel), ov, 0)
    return jnp.asarray(ov, BF16)


def _nsa_attention(qp, zp, xm, w_t, gate_block, kc, vct, kv, vt, bsz, seq):
    g, r, dv, dkp, tq, tk = NSA_GROUPS, NSA_HPG, NSA_DV, NSA_DKP, ATT_TQ, SEL_TK
    d = xm.shape[1]
    nq = seq // tq
    n_blk = seq // CMP_STRIDE
    n_sel = seq // SEL_LEN
    ovt = _overlap_matrix_t(n_blk, n_sel)
    row = lambda b, gg, i: b * nq + i
    z_spec = lambda br: pl.BlockSpec((tq, r * dv), lambda b, gg, i: (row(b, gg, i), br * g + gg))
    ks0, kw0 = 0, g * (dkp + dv) // dkp
    return pl.pallas_call(
        _nsa_attn_kernel,
        grid=(bsz, g, nq),
        in_specs=[pl.BlockSpec((tq, r * dkp), lambda b, gg, i: (row(b, gg, i), gg)),
                  z_spec(0), z_spec(1), z_spec(2),
                  pl.BlockSpec((tq, d), lambda b, gg, i: (row(b, gg, i), 0)),
                  pl.BlockSpec((LANES, d), lambda b, gg, i: (gate_block, 0)),
                  pl.BlockSpec((1, 1, n_blk, dkp), lambda b, gg, i: (b, gg, 0, 0)),
                  pl.BlockSpec((1, 1, V_ROWS, n_blk), lambda b, gg, i: (b, gg, 0, 0)),
                  pl.BlockSpec(ovt.shape, lambda b, gg, i: (0, 0)),
                  pl.BlockSpec((seq, dkp), lambda b, gg, i: (b, ks0 + gg)),
                  pl.BlockSpec((1, 1, V_ROWS, seq), lambda b, gg, i: (b, gg, 0, 0)),
                  pl.BlockSpec((seq, dkp), lambda b, gg, i: (b, kw0 + gg)),
                  pl.BlockSpec((1, 1, V_ROWS, seq), lambda b, gg, i: (b, g + gg, 0, 0))],
        out_specs=pl.BlockSpec((tq, r * dv), lambda b, gg, i: (row(b, gg, i), gg)),
        out_shape=jax.ShapeDtypeStruct((bsz * seq, g * r * dv), BF16),
        scratch_shapes=[pltpu.VMEM((n_sel, tq), F32),
                        pltpu.VMEM((tk, r * tq), F32),
                        pltpu.VMEM((tk, r * tq), F32),
                        pltpu.VMEM((1, r * tq), F32),
                        pltpu.VMEM((V_ROWS, r * tq), F32),
                        pltpu.VMEM((tq, r * dv), F32),
                        pltpu.VMEM((tq, r * dv), F32),
                        pltpu.VMEM((LANES, d), BF16),
                        pltpu.VMEM((LANES, tq), F32)],
        compiler_params=_params(3),
        name="nsa_attn",
    )(qp, zp, zp, zp, xm, w_t, kc, vct, ovt, kv, vt, kv, vt)


def _nsa_layer(x, xm, shared, w_in, w_out, gain, bias, bsz, seq, *, last):
    kc, vct, kv, vt = shared
    h, g, r, dk, dv, dkp = NSA_HEADS, NSA_GROUPS, NSA_HPG, NSA_DK, NSA_DV, NSA_DKP
    nq = h * dk
    nz = N_BRANCH * h * dv
    tn, tm = 1024, _proj_rows(xm)
    w_t = w_in.T
    qp = _proj(xm, w_t, n_tiles=g, col_block0=0, tm=tm, tn=r * dkp, tn_in=r * dk, out_dtype=BF16, w_transposed=True,
               place=_head_pad_matrix(r, dk, dkp, 0).T, out_scale=(dk ** -0.5) * math.log2(math.e))
    zp = _proj(xm, w_t, n_tiles=nz // tn, col_block0=nq // tn, tm=tm, tn=tn, out_dtype=BF16, w_transposed=True)
    mixed = _nsa_attention(qp, zp, xm, w_t, (nq + nz) // LANES, kc, vct, kv, vt, bsz, seq)
    return _out_ln(mixed, w_out.astype(BF16), x, gain, bias, with_bf16=not last)


def kernel(x, positions, ret_w_in_0, ret_w_out_0, ln_g_0, ln_b_0, ret_w_in_1, ret_w_out_1, ln_g_1, ln_b_1,
           nsa_w_kv, nsa_pe_k, nsa_pe_v, nsa_w_ck1, nsa_w_ck2, nsa_w_cv1, nsa_w_cv2,
           nsa_w_in_2, nsa_w_out_2, ln_g_2, ln_b_2, nsa_w_in_3, nsa_w_out_3, ln_g_3, ln_b_3):
    bsz, seq, d = x.shape
    h = x.reshape(bsz * seq, d)
    cos, sin = _rope_tables(positions)
    h, hm = _retention_layer(h, h, cos, sin, ret_w_in_0, ret_w_out_0, ln_g_0, ln_b_0, bsz, seq)
    h, hm = _retention_layer(h, hm, cos, sin, ret_w_in_1, ret_w_out_1, ln_g_1, ln_b_1, bsz, seq)
    shared = _nsa_shared_kv(hm, nsa_w_kv, nsa_pe_k, nsa_pe_v, nsa_w_ck1, nsa_w_ck2, nsa_w_cv1, nsa_w_cv2, bsz, seq)
    h, hm = _nsa_layer(h, hm, shared, nsa_w_in_2, nsa_w_out_2, ln_g_2, ln_b_2, bsz, seq, last=False)
    h = _nsa_layer(h, hm, shared, nsa_w_in_3, nsa_w_out_3, ln_g_3, ln_b_3, bsz, seq, last=True)
    return h.reshape(bsz, seq, d)
```

```python
import functools
import math

import numpy as np
import jax
import jax.numpy as jnp
from jax import lax
from jax.experimental import pallas as pl
from jax.experimental.pallas import tpu as pltpu

D_MODEL = 2048
DEPTH = 4
N_A_LAYERS = DEPTH // 2
RET_HEADS = 8
RET_DK = D_MODEL // RET_HEADS
RET_DV = 2 * D_MODEL // RET_HEADS
RET_CHUNK = 128
RET_THETA_BASE = 10000.0
NSA_HEADS = 16
NSA_GROUPS = 4
NSA_HPG = NSA_HEADS // NSA_GROUPS
NSA_DV = D_MODEL // NSA_HEADS
NSA_DK = 3 * NSA_DV // 2
NSA_DKP = 256
CMP_LEN = 32
CMP_STRIDE = 16
CMP_HID = 2 * NSA_DV
SEL_LEN = 64
SEL_TOPK = 16
WINDOW = 512
N_BRANCH = 3
ALPHA = (2.0 * DEPTH) ** 0.25
NEG_INF = -1e30
FORCE_SCORE = 1e9
LN_EPS = 1e-5

LANES = 128
SUBLANES = 8
BF16_SUBLANES = 16
ATT_TQ = 256
SEL_TK = 512
V_ROWS = NSA_DV + BF16_SUBLANES
VMEM_LIMIT = 56 * 1024 * 1024

F32 = jnp.float32
BF16 = jnp.bfloat16
NT_DIMS = (((1,), (1,)), ((), ()))
TN_DIMS = (((0,), (0,)), ((), ()))


def _params(n_axes):
    return pltpu.CompilerParams(dimension_semantics=("arbitrary",) * n_axes, vmem_limit_bytes=VMEM_LIMIT)


def _sigmoid(x):
    return 0.5 * jnp.tanh(0.5 * x) + 0.5


def _silu(x):
    half = 0.5 * x
    return half + half * jnp.tanh(half)


def _proj_kernel(x_ref, w_ref, *rest, out_scale, w_transposed, head_pad):
    perm_ref = rest[0] if len(rest) == 3 else None
    o_ref, wbf_ref = rest[-2:]

    @pl.when(pl.program_id(1) == 0)
    def _():
        if perm_ref is None:
            wbf_ref[...] = w_ref[...].astype(BF16)
        else:
            blk = perm_ref.shape[0]
            for c in range(w_ref.shape[1] // blk):
                cols = slice(c * blk, (c + 1) * blk)
                wbf_ref[:, cols] = jnp.dot(w_ref[:, cols].astype(BF16), perm_ref[...],
                                           preferred_element_type=F32).astype(BF16)

    xb = x_ref[...]
    if xb.dtype != BF16:
        xb = xb.astype(BF16)
    if w_transposed:
        acc = lax.dot_general(xb, wbf_ref[...], NT_DIMS, preferred_element_type=F32)
    else:
        acc = jnp.dot(xb, wbf_ref[...], preferred_element_type=F32)
    if out_scale is not None:
        acc = acc * out_scale
    if head_pad is not None:
        n_heads, width, padded = head_pad
        pieces = []
        for h in range(n_heads):
            pieces += [acc[:, h * width:(h + 1) * width], jnp.zeros((acc.shape[0], padded - width), F32)]
        if acc.shape[1] > n_heads * width:
            pieces.append(acc[:, n_heads * width:])
        acc = jnp.concatenate(pieces, axis=1)
    o_ref[...] = acc.astype(o_ref.dtype)


def _proj(x, w, *, n_tiles, col_block0, tm, tn, out_dtype, perm=None, out_scale=None, w_transposed=False,
          head_pad=None):
    m, k = x.shape
    tn_out = tn if head_pad is None else tn + head_pad[0] * (head_pad[2] - head_pad[1])
    if w_transposed:
        w_spec = pl.BlockSpec((tn, k), lambda j, i: (j + col_block0, 0))
    else:
        w_spec = pl.BlockSpec((k, tn), lambda j, i: (0, j + col_block0))
    in_specs = [pl.BlockSpec((tm, k), lambda j, i: (i, 0)), w_spec]
    args = [x, w]
    if perm is not None:
        in_specs.append(pl.BlockSpec(perm.shape, lambda j, i: (0, 0)))
        args.append(perm)
    return pl.pallas_call(
        functools.partial(_proj_kernel, out_scale=out_scale, w_transposed=w_transposed, head_pad=head_pad),
        grid=(n_tiles, m // tm),
        in_specs=in_specs,
        out_specs=pl.BlockSpec((tm, tn_out), lambda j, i: (i, j)),
        out_shape=jax.ShapeDtypeStruct((m, n_tiles * tn_out), out_dtype),
        scratch_shapes=[pltpu.VMEM((tn, k) if w_transposed else (k, tn), BF16)],
        compiler_params=_params(2),
        name="proj",
    )(*args)


def _out_ln_kernel(a_ref, w_ref, x_ref, g_ref, b_ref, o_ref, *o16_ref):
    u = ALPHA * x_ref[...] + jnp.dot(a_ref[...], w_ref[...], preferred_element_type=F32)
    mu = jnp.mean(u, axis=-1, keepdims=True)
    var = jnp.mean(jnp.square(u - mu), axis=-1, keepdims=True)
    y = (u - mu) * lax.rsqrt(var + LN_EPS) * g_ref[...] + b_ref[...]
    o_ref[...] = y
    if o16_ref:
        o16_ref[0][...] = y.astype(BF16)


def _out_ln(a, w, x, gain, bias, *, tm=512, with_bf16=True):
    m, ka = a.shape
    d = w.shape[1]
    o_spec = pl.BlockSpec((tm, d), lambda i: (i, 0))
    return pl.pallas_call(
        _out_ln_kernel,
        grid=(m // tm,),
        in_specs=[pl.BlockSpec((tm, ka), lambda i: (i, 0)),
                  pl.BlockSpec((ka, d), lambda i: (0, 0), pipeline_mode=pl.Buffered(1)),
                  pl.BlockSpec((tm, d), lambda i: (i, 0)),
                  pl.BlockSpec((1, d), lambda i: (0, 0)),
                  pl.BlockSpec((1, d), lambda i: (0, 0))],
        out_specs=[o_spec, o_spec] if with_bf16 else o_spec,
        out_shape=([jax.ShapeDtypeStruct((m, d), F32), jax.ShapeDtypeStruct((m, d), BF16)] if with_bf16
                   else jax.ShapeDtypeStruct((m, d), F32)),
        compiler_params=_params(1),
        name="out_ln",
    )(a, w, x, gain.reshape(1, d), bias.reshape(1, d))


def _rope_kernel(pos_ref, freq_ref, cos_ref, sin_ref):
    ang = pos_ref[...].astype(F32) * freq_ref[...]
    cos_ref[...] = jnp.cos(ang)
    sin_ref[...] = jnp.sin(ang)


def _rope_tables(positions, tm=1024):
    t = positions.size
    half = RET_DK // 2
    inv_freq = 1.0 / (RET_THETA_BASE ** jnp.linspace(0.0, 1.0, half, dtype=F32))
    return pl.pallas_call(
        _rope_kernel,
        grid=(t // tm,),
        in_specs=[pl.BlockSpec((tm, 1), lambda i: (i, 0)), pl.BlockSpec((1, half), lambda i: (0, 0))],
        out_specs=[pl.BlockSpec((tm, half), lambda i: (i, 0))] * 2,
        out_shape=[jax.ShapeDtypeStruct((t, half), F32)] * 2,
        compiler_params=_params(1),
        name="rope_tables",
    )(positions.reshape(t, 1), inv_freq.reshape(1, half))


def _retention_kernel(q_ref, k_ref, v_ref, z_ref, cos_ref, sin_ref, dintra_ref, dq_ref, dk_ref, dch_ref,
                      o_ref, state_ref, *, n_sub, n_heads):
    c = RET_CHUNK
    half = RET_DK // 2

    @pl.when(pl.program_id(2) == 0)
    def _():
        state_ref[...] = jnp.zeros_like(state_ref)

    def rotate(t, cosv, sinv):
        even, odd = t[:, :half], t[:, half:]
        return jnp.concatenate([even * cosv - odd * sinv, even * sinv + odd * cosv], axis=1)

    for j in range(n_sub):
        rows = pl.ds(j * c, c)
        cosv, sinv = cos_ref[rows, :], sin_ref[rows, :]
        for hh in range(n_heads):
            kcols = slice(hh * RET_DK, (hh + 1) * RET_DK)
            vcols = slice(hh * RET_DV, (hh + 1) * RET_DV)
            qr = rotate(q_ref[rows, kcols].astype(F32), cosv, sinv)
            kr = rotate(k_ref[rows, kcols].astype(F32), cosv, sinv)
            qb, kb = qr.astype(BF16), kr.astype(BF16)
            kdb = (kr * dk_ref[hh]).astype(BF16)
            v = v_ref[rows, vcols]
            state = state_ref[hh]
            scores = lax.dot_general(qb, kb, NT_DIMS, preferred_element_type=F32) * dintra_ref[hh]
            out = (jnp.dot(scores.astype(BF16), v, preferred_element_type=F32)
                   + jnp.dot(qb, state.astype(BF16), preferred_element_type=F32) * dq_ref[hh])
            state_ref[hh] = state * dch_ref[hh] + lax.dot_general(kdb, v, TN_DIMS, preferred_element_type=F32)
            mu = jnp.mean(out, axis=-1, keepdims=True)
            var = jnp.mean(jnp.square(out - mu), axis=-1, keepdims=True)
            normed = (out - mu) * lax.rsqrt(var + LN_EPS)
            o_ref[rows, vcols] = (normed * _silu(z_ref[rows, vcols].astype(F32))).astype(BF16)


def _retention_decays():
    h, c = RET_HEADS, RET_CHUNK
    k_scale = RET_DK ** -0.5
    assert math.frexp(k_scale)[0] == 0.5
    log_gamma = jnp.log1p(-jnp.exp2(-5.0 - jnp.arange(h, dtype=F32)))
    idx = jnp.arange(c, dtype=F32)
    rel = idx[:, None] - idx[None, :]
    dintra = jnp.where(rel >= 0, jnp.exp(log_gamma[:, None, None] * jnp.maximum(rel, 0.0)), 0.0) * k_scale
    dq = jnp.exp(log_gamma[:, None] * (idx + 1.0))
    dk = jnp.exp(log_gamma[:, None] * (c - 1.0 - idx)) * k_scale
    dch = jnp.exp(log_gamma * c)
    return (dintra,
            jnp.broadcast_to(dq[:, :, None], (h, c, RET_DV)),
            jnp.broadcast_to(dk[:, :, None], (h, c, RET_DK)),
            jnp.broadcast_to(dch[:, None, None], (h, 1, RET_DV)))


def _retention(qk, vz, cos, sin, bsz, seq, *, ts=1024, hps=4):
    h, dk, dv, c = RET_HEADS, RET_DK, RET_DV, RET_CHUNK
    ns = seq // ts
    ng = h // hps
    dintra, dq, dkt, dch = _retention_decays()
    row = lambda b, hh, s: b * ns + s
    return pl.pallas_call(
        functools.partial(_retention_kernel, n_sub=ts // c, n_heads=hps),
        grid=(bsz, ng, ns),
        in_specs=[pl.BlockSpec((ts, hps * dk), lambda b, hh, s: (row(b, hh, s), hh)),
                  pl.BlockSpec((ts, hps * dk), lambda b, hh, s: (row(b, hh, s), ng + hh)),
                  pl.BlockSpec((ts, hps * dv), lambda b, hh, s: (row(b, hh, s), hh)),
                  pl.BlockSpec((ts, hps * dv), lambda b, hh, s: (row(b, hh, s), ng + hh)),
                  pl.BlockSpec((ts, dk // 2), lambda b, hh, s: (row(b, hh, s), 0)),
                  pl.BlockSpec((ts, dk // 2), lambda b, hh, s: (row(b, hh, s), 0)),
                  pl.BlockSpec((hps, c, c), lambda b, hh, s: (hh, 0, 0)),
                  pl.BlockSpec((hps, c, dv), lambda b, hh, s: (hh, 0, 0)),
                  pl.BlockSpec((hps, c, dk), lambda b, hh, s: (hh, 0, 0)),
                  pl.BlockSpec((hps, 1, dv), lambda b, hh, s: (hh, 0, 0))],
        out_specs=pl.BlockSpec((ts, hps * dv), lambda b, hh, s: (row(b, hh, s), hh)),
        out_shape=jax.ShapeDtypeStruct((bsz * seq, h * dv), BF16),
        scratch_shapes=[pltpu.VMEM((hps, dk, dv), F32)],
        compiler_params=_params(3),
        name="retention",
    )(qk, qk, vz, vz, cos, sin, dintra, dq, dkt, dch)


def _deinterleave_matrix(width):
    p = np.zeros((width, width), np.float32)
    j = np.arange(width // 2)
    p[2 * j, j] = 1.0
    p[2 * j + 1, width // 2 + j] = 1.0
    return jnp.asarray(p, BF16)


def _proj_rows(xm):
    return 2048 if xm.dtype == BF16 else 1024


def _retention_layer(x, xm, cos, sin, w_in, w_out, gain, bias, bsz, seq):
    h, dk, dv = RET_HEADS, RET_DK, RET_DV
    tn, tm = 1024, _proj_rows(xm)
    qk = _proj(xm, w_in, n_tiles=2 * h * dk // tn, col_block0=0, tm=tm, tn=tn, out_dtype=BF16,
               perm=_deinterleave_matrix(dk))
    vz = _proj(xm, w_in, n_tiles=2 * h * dv // tn, col_block0=2 * h * dk // tn, tm=tm, tn=tn, out_dtype=BF16)
    mixed = _retention(qk, vz, cos, sin, bsz, seq)
    return _out_ln(mixed, w_out.astype(BF16), x, gain, bias)


def _compress_kernel(*refs, transposed):
    x_refs, (pe_ref, w1_ref, w2_ref, o_ref) = refs[:-4], refs[-4:]
    n_blk = o_ref.shape[3] if transposed else o_ref.shape[2]
    half = CMP_LEN // 2
    acc_lo = jnp.zeros((n_blk, CMP_HID), F32)
    acc_hi = jnp.zeros((n_blk, CMP_HID), F32)
    for l in range(half):
        rows = jnp.concatenate([x_ref[pl.ds(l, n_blk, stride=CMP_STRIDE), :] for x_ref in x_refs], axis=1)
        acc_lo += jnp.dot((rows + pe_ref[l:l + 1, :]).astype(BF16), w1_ref[l], preferred_element_type=F32)
        acc_hi += jnp.dot((rows + pe_ref[half + l:half + l + 1, :]).astype(BF16), w1_ref[half + l],
                          preferred_element_type=F32)
    hid = acc_lo + pltpu.roll(acc_hi, shift=n_blk - 1, axis=0)
    out = jnp.dot(_silu(hid).astype(BF16), w2_ref[...], preferred_element_type=F32)
    keep = lax.broadcasted_iota(jnp.int32, out.shape, 0) < n_blk - 1
    out = jnp.where(keep, out, 0.0)
    if transposed:
        width = out.shape[1]
        o_ref[0, 0, :width, :] = out.T.astype(BF16)
        o_ref[0, 0, width:, :] = jnp.ones((o_ref.shape[2] - width, n_blk), BF16)
    else:
        o_ref[0, 0] = out.astype(BF16)


def _compress(kvc, pe, w1, w2, bsz, seq, *, width, col_block0, out_rows=None):
    g = NSA_GROUPS
    n_blk = seq // CMP_STRIDE
    n_slabs = width // LANES
    out_block = (1, 1, n_blk, w2.shape[1]) if out_rows is None else (1, 1, out_rows, n_blk)
    slab = lambda j: pl.BlockSpec((seq, LANES), lambda b, gg: (b, (col_block0 + gg) * n_slabs + j))
    return pl.pallas_call(
        functools.partial(_compress_kernel, transposed=out_rows is not None),
        grid=(bsz, g),
        in_specs=[slab(j) for j in range(n_slabs)] + [
                  pl.BlockSpec(pe.shape, lambda b, gg: (0, 0)),
                  pl.BlockSpec(w1.shape, lambda b, gg: (0, 0, 0)),
                  pl.BlockSpec(w2.shape, lambda b, gg: (0, 0))],
        out_specs=pl.BlockSpec(out_block, lambda b, gg: (b, gg, 0, 0)),
        out_shape=jax.ShapeDtypeStruct((bsz, g) + out_block[2:], BF16),
        compiler_params=_params(2),
        name="compress",
    )(*([kvc] * n_slabs), pe, w1, w2)


def _values_t_kernel(v_ref, o_ref):
    o_ref[0, 0, :NSA_DV, :] = v_ref[...].astype(F32).T.astype(BF16)
    o_ref[0, 0, NSA_DV:, :] = jnp.ones((V_ROWS - NSA_DV, v_ref.shape[0]), BF16)


def _values_t(kv, bsz, seq):
    g, dv, dkp = NSA_GROUPS, NSA_DV, NSA_DKP
    branch_blocks = g * (dkp + dv) // dv
    v0 = g * dkp // dv
    return pl.pallas_call(
        _values_t_kernel,
        grid=(bsz, 2 * g),
        in_specs=[pl.BlockSpec((seq, dv), lambda b, j: (b, v0 + (j // g) * branch_blocks + j % g))],
        out_specs=pl.BlockSpec((1, 1, V_ROWS, seq), lambda b, j: (b, j, 0, 0)),
        out_shape=jax.ShapeDtypeStruct((bsz, 2 * g, V_ROWS, seq), BF16),
        compiler_params=_params(2),
        name="values_t",
    )(kv)


def _pad_last(a, width):
    return jnp.pad(a, [(0, 0)] * (a.ndim - 1) + [(0, width - a.shape[-1])])


def _nsa_shared_kv(x, w_kv, pe_k, pe_v, w_ck1, w_ck2, w_cv1, w_cv2, bsz, seq):
    g, dk, dv, dkp = NSA_GROUPS, NSA_DK, NSA_DV, NSA_DKP
    tn = g * (dk + dv)
    pad = (g, dk, dkp)
    kvc = _proj(x, w_kv, n_tiles=1, col_block0=0, tm=1024, tn=tn, out_dtype=F32, head_pad=pad)
    kv = _proj(x, w_kv, n_tiles=2, col_block0=1, tm=1024, tn=tn, out_dtype=BF16, head_pad=pad)
    w1k = _pad_last(w_ck1.reshape(CMP_LEN, dk, CMP_HID).transpose(0, 2, 1), dkp).transpose(0, 2, 1).astype(BF16)
    w1v = w_cv1.reshape(CMP_LEN, dv, CMP_HID).astype(BF16)
    kc = _compress(kvc, _pad_last(pe_k, dkp), w1k, _pad_last(w_ck2, dkp).astype(BF16), bsz, seq,
                   width=dkp, col_block0=0)
    vct = _compress(kvc, pe_v, w1v, w_cv2.astype(BF16), bsz, seq, width=dv, col_block0=g * dkp // dv,
                    out_rows=V_ROWS)
    return kc, vct, kv, _values_t(kv, bsz, seq)


def _nsa_attn_kernel(q_ref, zc_ref, zs_ref, zw_ref, x_ref, wg_ref, kc_ref, vct_ref, ovt_ref,
                     ks_ref, vst_ref, kw_ref, vwt_ref, o_ref,
                     bias_ref, sa_ref, sb_ref, m_ref, acc_ref, mix_ref, zsil_ref, wgb_ref, gates_ref):
    tq, r, dv, tk = q_ref.shape[0], NSA_HPG, NSA_DV, SEL_TK
    lanes = r * tq
    n_blk = kc_ref.shape[2]
    n_sel = n_blk * CMP_STRIDE // SEL_LEN
    q0 = pl.program_id(2) * tq
    q4 = jnp.concatenate([q_ref[:, h * NSA_DKP:(h + 1) * NSA_DKP] for h in range(r)], axis=0)

    def qpos(shape):
        return q0 + lax.rem(lax.broadcasted_iota(jnp.int32, shape, 1), tq)

    def all_heads(tile):
        return jnp.concatenate([tile] * r, axis=1)

    def weights(s):
        return jnp.exp2(s - jnp.max(s, axis=0, keepdims=True)).astype(BF16)

    @pl.when((pl.program_id(0) == 0) & (pl.program_id(1) == 0) & (pl.program_id(2) == 0))
    def _():
        real = lax.broadcasted_iota(jnp.int32, wg_ref.shape, 0) < N_BRANCH * NSA_HEADS
        wgb_ref[...] = jnp.where(real, wg_ref[...], 0.0).astype(BF16)

    gates_ref[...] = _sigmoid(lax.dot_general(wgb_ref[...], x_ref[...], NT_DIMS, preferred_element_type=F32))
    head0 = pl.program_id(1) * r

    def gated(o_aug, inv_l, silu_z, br, h):
        cols = slice(h * tq, (h + 1) * tq)
        row_scale = gates_ref[pl.ds(br * NSA_HEADS + head0 + h, 1), :] * inv_l[:, cols]
        return (o_aug[:dv, cols] * row_scale).T * silu_z

    s = lax.dot_general(kc_ref[0, 0], q4, NT_DIMS, preferred_element_type=F32)
    blk_end = lax.broadcasted_iota(jnp.int32, (n_blk, tq), 0) * CMP_STRIDE + (CMP_LEN - 1)
    s = s + all_heads(jnp.where(blk_end <= qpos((n_blk, tq)), 0.0, NEG_INF))
    eb = weights(s)
    o_cmp = jnp.dot(vct_ref[0, 0], eb, preferred_element_type=F32)
    inv_c = jnp.where(qpos((1, lanes)) >= CMP_LEN - 1, 1.0 / o_cmp[dv:dv + 1, :], 0.0)

    imp = sum(jnp.dot(ovt_ref[...], eb[:, h * tq:(h + 1) * tq], preferred_element_type=F32)
              * inv_c[:, h * tq:(h + 1) * tq] for h in range(r))
    imp = imp[:n_sel, :]
    blk = lax.broadcasted_iota(jnp.int32, imp.shape, 0)
    cur = (q0 + lax.broadcasted_iota(jnp.int32, imp.shape, 1)) // SEL_LEN
    forced = (blk == 0) | (blk == cur) | (blk == cur - 1)
    score = jnp.where(forced, FORCE_SCORE, jnp.where(blk <= cur, imp, -1.0))
    groups = [score[g0:g0 + SUBLANES, :] for g0 in range(0, n_sel, SUBLANES)]
    ranks = [jnp.zeros((SUBLANES, tq), F32) for _ in groups]
    for i in range(n_sel):
        row = score[i:i + 1, :]
        for gi, grp in enumerate(groups):
            g0 = gi * SUBLANES
            if g0 > i:
                ahead = row >= grp
            elif g0 + SUBLANES - 1 < i:
                ahead = row > grp
            else:
                after = lax.broadcasted_iota(jnp.int32, grp.shape, 0) + g0 > i
                ahead = (row > grp) | (after & (row == grp))
            ranks[gi] = ranks[gi] + jnp.where(ahead, 1.0, 0.0)
    rank = jnp.concatenate(ranks, axis=0)
    bias_ref[...] = jnp.where(rank < float(min(SEL_TOPK, n_sel)), 0.0, NEG_INF)

    span = WINDOW + tq
    k0 = pl.multiple_of(jnp.maximum(q0 - WINDOW, 0), tq)
    s = lax.dot_general(kw_ref[pl.ds(k0, span), :], q4, NT_DIMS, preferred_element_type=F32)
    dist = qpos((span, tq)) - (k0 + lax.broadcasted_iota(jnp.int32, (span, tq), 0))
    s = s + all_heads(jnp.where((dist >= 0) & (dist < WINDOW), 0.0, NEG_INF))
    o_win = jnp.dot(vwt_ref[0, 0, :, pl.ds(k0, span)], weights(s), preferred_element_type=F32)
    inv_w = 1.0 / o_win[dv:dv + 1, :]

    for h in range(r):
        cols = slice(h * dv, (h + 1) * dv)
        mix_ref[:, cols] = (gated(o_cmp, inv_c, _silu(zc_ref[:, cols].astype(F32)), 0, h)
                            + gated(o_win, inv_w, _silu(zw_ref[:, cols].astype(F32)), 2, h))
    zsil_ref[...] = _silu(zs_ref[...].astype(F32))

    m_ref[...] = jnp.full(m_ref.shape, NEG_INF, F32)
    acc_ref[...] = jnp.zeros_like(acc_ref)
    blocks_per_tile = tk // SEL_LEN
    n_tiles = q0 // tk + 1
    causal_gap = (q0 + lax.broadcasted_iota(jnp.int32, (tk, tq), 1)) - lax.broadcasted_iota(jnp.int32, (tk, tq), 0)

    def scores(tile):
        kbase = pl.multiple_of(tile * tk, tk)
        rows = bias_ref[pl.ds(pl.multiple_of(tile * blocks_per_tile, blocks_per_tile), blocks_per_tile), :]
        bias = jnp.concatenate([jnp.broadcast_to(rows[j:j + 1, :], (SEL_LEN, tq)) for j in range(blocks_per_tile)],
                               axis=0)
        bias = jnp.where(causal_gap >= kbase, bias, NEG_INF)
        return (lax.dot_general(ks_ref[pl.ds(kbase, tk), :], q4, NT_DIMS, preferred_element_type=F32)
                + all_heads(bias))

    def attend(s_ref, tile):
        kbase = pl.multiple_of(tile * tk, tk)
        s = s_ref[...]
        m_prev = m_ref[...]
        m_new = jnp.maximum(m_prev, jnp.max(s, axis=0, keepdims=True))
        pv = jnp.dot(vst_ref[0, 0, :, pl.ds(kbase, tk)], jnp.exp2(s - m_new).astype(BF16),
                     preferred_element_type=F32)
        acc_ref[...] = jnp.exp2(m_prev - m_new) * acc_ref[...] + pv
        m_ref[...] = m_new

    sa_ref[...] = scores(0)

    def pair(jp, carry):
        first = 2 * jp
        sb_ref[...] = scores(first + 1)
        attend(sa_ref, first)
        sa_ref[...] = scores(jnp.minimum(first + 2, n_tiles - 1))
        attend(sb_ref, first + 1)
        return carry

    lax.fori_loop(0, n_tiles // 2, pair, 0)

    @pl.when(lax.rem(n_tiles, 2) == 1)
    def _():
        attend(sa_ref, n_tiles - 1)

    o_sel = acc_ref[...]
    inv_s = 1.0 / o_sel[dv:dv + 1, :]
    for h in range(r):
        cols = slice(h * dv, (h + 1) * dv)
        o_ref[:, cols] = (mix_ref[:, cols] + gated(o_sel, inv_s, zsil_ref[:, cols], 1, h)).astype(BF16)


def _overlap_matrix_t(n_blk, n_sel):
    a, b = SEL_LEN // CMP_STRIDE, CMP_LEN // CMP_STRIDE
    span = a + b - 2
    n = np.arange(n_blk)[None, :]
    j = np.arange(LANES)[:, None]
    diff = n - a * j
    ov = np.where((diff >= 0) & (diff <= span), np.minimum(np.minimum(diff, span - diff), min(a, b) - 1) + 1, 0)
    ov = np.where((n < n_blk - 1) & (j < n_sel), ov, 0)
    return jnp.asarray(ov, BF16)


def _nsa_attention(qp, zp, xm, w_t, gate_block, kc, vct, kv, vt, bsz, seq):
    g, r, dv, dkp, tq, tk = NSA_GROUPS, NSA_HPG, NSA_DV, NSA_DKP, ATT_TQ, SEL_TK
    d = xm.shape[1]
    nq = seq // tq
    n_blk = seq // CMP_STRIDE
    n_sel = seq // SEL_LEN
    ovt = _overlap_matrix_t(n_blk, n_sel)
    row = lambda b, gg, i: b * nq + i
    z_spec = lambda br: pl.BlockSpec((tq, r * dv), lambda b, gg, i: (row(b, gg, i), br * g + gg))
    ks0, kw0 = 0, g * (dkp + dv) // dkp
    return pl.pallas_call(
        _nsa_attn_kernel,
        grid=(bsz, g, nq),
        in_specs=[pl.BlockSpec((tq, r * dkp), lambda b, gg, i: (row(b, gg, i), gg)),
                  z_spec(0), z_spec(1), z_spec(2),
                  pl.BlockSpec((tq, d), lambda b, gg, i: (row(b, gg, i), 0)),
                  pl.BlockSpec((LANES, d), lambda b, gg, i: (gate_block, 0)),
                  pl.BlockSpec((1, 1, n_blk, dkp), lambda b, gg, i: (b, gg, 0, 0)),
                  pl.BlockSpec((1, 1, V_ROWS, n_blk), lambda b, gg, i: (b, gg, 0, 0)),
                  pl.BlockSpec(ovt.shape, lambda b, gg, i: (0, 0)),
                  pl.BlockSpec((seq, dkp), lambda b, gg, i: (b, ks0 + gg)),
                  pl.BlockSpec((1, 1, V_ROWS, seq), lambda b, gg, i: (b, gg, 0, 0)),
                  pl.BlockSpec((seq, dkp), lambda b, gg, i: (b, kw0 + gg)),
                  pl.BlockSpec((1, 1, V_ROWS, seq), lambda b, gg, i: (b, g + gg, 0, 0))],
        out_specs=pl.BlockSpec((tq, r * dv), lambda b, gg, i: (row(b, gg, i), gg)),
        out_shape=jax.ShapeDtypeStruct((bsz * seq, g * r * dv), BF16),
        scratch_shapes=[pltpu.VMEM((n_sel, tq), F32),
                        pltpu.VMEM((tk, r * tq), F32),
                        pltpu.VMEM((tk, r * tq), F32),
                        pltpu.VMEM((1, r * tq), F32),
                        pltpu.VMEM((V_ROWS, r * tq), F32),
                        pltpu.VMEM((tq, r * dv), F32),
                        pltpu.VMEM((tq, r * dv), F32),
                        pltpu.VMEM((LANES, d), BF16),
                        pltpu.VMEM((LANES, tq), F32)],
        compiler_params=_params(3),
        name="nsa_attn",
    )(qp, zp, zp, zp, xm, w_t, kc, vct, ovt, kv, vt, kv, vt)


def _nsa_layer(x, xm, shared, w_in, w_out, gain, bias, bsz, seq, *, last):
    kc, vct, kv, vt = shared
    h, g, r, dk, dv, dkp = NSA_HEADS, NSA_GROUPS, NSA_HPG, NSA_DK, NSA_DV, NSA_DKP
    nq = h * dk
    nz = N_BRANCH * h * dv
    tn, tm = 1024, _proj_rows(xm)
    w_t = w_in.T
    qp = _proj(xm, w_t, n_tiles=g, col_block0=0, tm=tm, tn=r * dk, out_dtype=BF16, w_transposed=True,
               head_pad=(r, dk, dkp), out_scale=(dk ** -0.5) * math.log2(math.e))
    zp = _proj(xm, w_t, n_tiles=nz // tn, col_block0=nq // tn, tm=tm, tn=tn, out_dtype=BF16, w_transposed=True)
    mixed = _nsa_attention(qp, zp, xm, w_t, (nq + nz) // LANES, kc, vct, kv, vt, bsz, seq)
    return _out_ln(mixed, w_out.astype(BF16), x, gain, bias, with_bf16=not last)


def kernel(x, positions, ret_w_in_0, ret_w_out_0, ln_g_0, ln_b_0, ret_w_in_1, ret_w_out_1, ln_g_1, ln_b_1,
           nsa_w_kv, nsa_pe_k, nsa_pe_v, nsa_w_ck1, nsa_w_ck2, nsa_w_cv1, nsa_w_cv2,
           nsa_w_in_2, nsa_w_out_2, ln_g_2, ln_b_2, nsa_w_in_3, nsa_w_out_3, ln_g_3, ln_b_3):
    bsz, seq, d = x.shape
    h = x.reshape(bsz * seq, d)
    cos, sin = _rope_tables(positions)
    h, hm = _retention_layer(h, h, cos, sin, ret_w_in_0, ret_w_out_0, ln_g_0, ln_b_0, bsz, seq)
    h, hm = _retention_layer(h, hm, cos, sin, ret_w_in_1, ret_w_out_1, ln_g_1, ln_b_1, bsz, seq)
    shared = _nsa_shared_kv(hm, nsa_w_kv, nsa_pe_k, nsa_pe_v, nsa_w_ck1, nsa_w_ck2, nsa_w_cv1, nsa_w_cv2, bsz, seq)
    h, hm = _nsa_layer(h, hm, shared, nsa_w_in_2, nsa_w_out_2, ln_g_2, ln_b_2, bsz, seq, last=False)
    h = _nsa_layer(h, hm, shared, nsa_w_in_3, nsa_w_out_3, ln_g_3, ln_b_3, bsz, seq, last=True)
    return h.reshape(bsz, seq, d)
```

```python
import functools
import math

import numpy as np
import jax
import jax.numpy as jnp
from jax import lax
from jax.experimental import pallas as pl
from jax.experimental.pallas import tpu as pltpu

D_MODEL = 2048
DEPTH = 4
N_A_LAYERS = DEPTH // 2
RET_HEADS = 8
RET_DK = D_MODEL // RET_HEADS
RET_DV = 2 * D_MODEL // RET_HEADS
RET_CHUNK = 128
RET_THETA_BASE = 10000.0
NSA_HEADS = 16
NSA_GROUPS = 4
NSA_HPG = NSA_HEADS // NSA_GROUPS
NSA_DV = D_MODEL // NSA_HEADS
NSA_DK = 3 * NSA_DV // 2
NSA_DKP = 256
CMP_LEN = 32
CMP_STRIDE = 16
CMP_HID = 2 * NSA_DV
SEL_LEN = 64
SEL_TOPK = 16
WINDOW = 512
N_BRANCH = 3
ALPHA = (2.0 * DEPTH) ** 0.25
NEG_INF = -1e30
FORCE_SCORE = 1e9
LN_EPS = 1e-5

LANES = 128
SUBLANES = 8
BF16_SUBLANES = 16
ATT_TQ = 256
SEL_TK = 512
V_ROWS = NSA_DV + BF16_SUBLANES
VMEM_LIMIT = 56 * 1024 * 1024

F32 = jnp.float32
BF16 = jnp.bfloat16
NT_DIMS = (((1,), (1,)), ((), ()))
TN_DIMS = (((0,), (0,)), ((), ()))


def _params(n_axes):
    return pltpu.CompilerParams(dimension_semantics=("arbitrary",) * n_axes, vmem_limit_bytes=VMEM_LIMIT)


def _sigmoid(x):
    return 0.5 * jnp.tanh(0.5 * x) + 0.5


def _silu(x):
    half = 0.5 * x
    return half + half * jnp.tanh(half)


def _proj_kernel(x_ref, w_ref, *rest, out_scale, w_transposed, head_pad):
    perm_ref = rest[0] if len(rest) == 3 else None
    o_ref, wbf_ref = rest[-2:]

    @pl.when(pl.program_id(1) == 0)
    def _():
        if perm_ref is None:
            wbf_ref[...] = w_ref[...].astype(BF16)
        else:
            blk = perm_ref.shape[0]
            for c in range(w_ref.shape[1] // blk):
                cols = slice(c * blk, (c + 1) * blk)
                wbf_ref[:, cols] = jnp.dot(w_ref[:, cols].astype(BF16), perm_ref[...],
                                           preferred_element_type=F32).astype(BF16)

    xb = x_ref[...]
    if xb.dtype != BF16:
        xb = xb.astype(BF16)
    if w_transposed:
        acc = lax.dot_general(xb, wbf_ref[...], NT_DIMS, preferred_element_type=F32)
    else:
        acc = jnp.dot(xb, wbf_ref[...], preferred_element_type=F32)
    if out_scale is not None:
        acc = acc * out_scale
    if head_pad is not None:
        n_heads, width, padded = head_pad
        pieces = []
        for h in range(n_heads):
            pieces += [acc[:, h * width:(h + 1) * width], jnp.zeros((acc.shape[0], padded - width), F32)]
        if acc.shape[1] > n_heads * width:
            pieces.append(acc[:, n_heads * width:])
        acc = jnp.concatenate(pieces, axis=1)
    o_ref[...] = acc.astype(o_ref.dtype)


def _proj(x, w, *, n_tiles, col_block0, tm, tn, out_dtype, perm=None, out_scale=None, w_transposed=False,
          head_pad=None):
    m, k = x.shape
    tn_out = tn if head_pad is None else tn + head_pad[0] * (head_pad[2] - head_pad[1])
    if w_transposed:
        w_spec = pl.BlockSpec((tn, k), lambda j, i: (j + col_block0, 0))
    else:
        w_spec = pl.BlockSpec((k, tn), lambda j, i: (0, j + col_block0))
    in_specs = [pl.BlockSpec((tm, k), lambda j, i: (i, 0)), w_spec]
    args = [x, w]
    if perm is not None:
        in_specs.append(pl.BlockSpec(perm.shape, lambda j, i: (0, 0)))
        args.append(perm)
    return pl.pallas_call(
        functools.partial(_proj_kernel, out_scale=out_scale, w_transposed=w_transposed, head_pad=head_pad),
        grid=(n_tiles, m // tm),
        in_specs=in_specs,
        out_specs=pl.BlockSpec((tm, tn_out), lambda j, i: (i, j)),
        out_shape=jax.ShapeDtypeStruct((m, n_tiles * tn_out), out_dtype),
        scratch_shapes=[pltpu.VMEM((tn, k) if w_transposed else (k, tn), BF16)],
        compiler_params=_params(2),
        name="proj",
    )(*args)


OUT_LN_SPLIT = 4


def _out_ln_kernel(a_ref, w_ref, x_ref, g_ref, b_ref, o_ref, *o16_ref):
    rows = a_ref.shape[0] // OUT_LN_SPLIT
    for part in range(OUT_LN_SPLIT):
        sl = slice(part * rows, (part + 1) * rows)
        u = ALPHA * x_ref[sl, :] + jnp.dot(a_ref[sl, :], w_ref[...], preferred_element_type=F32)
        mu = jnp.mean(u, axis=-1, keepdims=True)
        var = jnp.mean(jnp.square(u - mu), axis=-1, keepdims=True)
        y = (u - mu) * lax.rsqrt(var + LN_EPS) * g_ref[...] + b_ref[...]
        o_ref[sl, :] = y
        if o16_ref:
            o16_ref[0][sl, :] = y.astype(BF16)


def _out_ln(a, w, x, gain, bias, *, tm=512, with_bf16=True):
    m, ka = a.shape
    d = w.shape[1]
    o_spec = pl.BlockSpec((tm, d), lambda i: (i, 0))
    return pl.pallas_call(
        _out_ln_kernel,
        grid=(m // tm,),
        in_specs=[pl.BlockSpec((tm, ka), lambda i: (i, 0)),
                  pl.BlockSpec((ka, d), lambda i: (0, 0), pipeline_mode=pl.Buffered(1)),
                  pl.BlockSpec((tm, d), lambda i: (i, 0)),
                  pl.BlockSpec((1, d), lambda i: (0, 0)),
                  pl.BlockSpec((1, d), lambda i: (0, 0))],
        out_specs=[o_spec, o_spec] if with_bf16 else o_spec,
        out_shape=([jax.ShapeDtypeStruct((m, d), F32), jax.ShapeDtypeStruct((m, d), BF16)] if with_bf16
                   else jax.ShapeDtypeStruct((m, d), F32)),
        compiler_params=_params(1),
        name="out_ln",
    )(a, w, x, gain.reshape(1, d), bias.reshape(1, d))


def _rope_kernel(pos_ref, freq_ref, cos_ref, sin_ref):
    ang = pos_ref[...].astype(F32) * freq_ref[...]
    cos_ref[...] = jnp.cos(ang)
    sin_ref[...] = jnp.sin(ang)


def _rope_tables(positions, tm=1024):
    t = positions.size
    half = RET_DK // 2
    inv_freq = 1.0 / (RET_THETA_BASE ** jnp.linspace(0.0, 1.0, half, dtype=F32))
    return pl.pallas_call(
        _rope_kernel,
        grid=(t // tm,),
        in_specs=[pl.BlockSpec((tm, 1), lambda i: (i, 0)), pl.BlockSpec((1, half), lambda i: (0, 0))],
        out_specs=[pl.BlockSpec((tm, half), lambda i: (i, 0))] * 2,
        out_shape=[jax.ShapeDtypeStruct((t, half), F32)] * 2,
        compiler_params=_params(1),
        name="rope_tables",
    )(positions.reshape(t, 1), inv_freq.reshape(1, half))


def _retention_kernel(q_ref, k_ref, v_ref, z_ref, cos_ref, sin_ref, dintra_ref, dq_ref, dk_ref, dch_ref,
                      o_ref, state_ref, *, n_sub, n_heads):
    c = RET_CHUNK
    half = RET_DK // 2

    @pl.when(pl.program_id(2) == 0)
    def _():
        state_ref[...] = jnp.zeros_like(state_ref)

    def rotate(t, cosv, sinv):
        even, odd = t[:, :half], t[:, half:]
        return jnp.concatenate([even * cosv - odd * sinv, even * sinv + odd * cosv], axis=1)

    for j in range(n_sub):
        rows = pl.ds(j * c, c)
        cosv, sinv = cos_ref[rows, :], sin_ref[rows, :]
        for hh in range(n_heads):
            kcols = slice(hh * RET_DK, (hh + 1) * RET_DK)
            vcols = slice(hh * RET_DV, (hh + 1) * RET_DV)
            qr = rotate(q_ref[rows, kcols].astype(F32), cosv, sinv)
            kr = rotate(k_ref[rows, kcols].astype(F32), cosv, sinv)
            qb, kb = qr.astype(BF16), kr.astype(BF16)
            kdb = (kr * dk_ref[hh]).astype(BF16)
            v = v_ref[rows, vcols]
            state = state_ref[hh]
            scores = lax.dot_general(qb, kb, NT_DIMS, preferred_element_type=F32) * dintra_ref[hh]
            out = (jnp.dot(scores.astype(BF16), v, preferred_element_type=F32)
                   + jnp.dot(qb, state.astype(BF16), preferred_element_type=F32) * dq_ref[hh])
            state_ref[hh] = state * dch_ref[hh] + lax.dot_general(kdb, v, TN_DIMS, preferred_element_type=F32)
            mu = jnp.mean(out, axis=-1, keepdims=True)
            var = jnp.mean(jnp.square(out - mu), axis=-1, keepdims=True)
            normed = (out - mu) * lax.rsqrt(var + LN_EPS)
            o_ref[rows, vcols] = (normed * _silu(z_ref[rows, vcols].astype(F32))).astype(BF16)


def _retention_decays():
    h, c = RET_HEADS, RET_CHUNK
    k_scale = RET_DK ** -0.5
    assert math.frexp(k_scale)[0] == 0.5
    log_gamma = jnp.log1p(-jnp.exp2(-5.0 - jnp.arange(h, dtype=F32)))
    idx = jnp.arange(c, dtype=F32)
    rel = idx[:, None] - idx[None, :]
    dintra = jnp.where(rel >= 0, jnp.exp(log_gamma[:, None, None] * jnp.maximum(rel, 0.0)), 0.0) * k_scale
    dq = jnp.exp(log_gamma[:, None] * (idx + 1.0))
    dk = jnp.exp(log_gamma[:, None] * (c - 1.0 - idx)) * k_scale
    dch = jnp.exp(log_gamma * c)
    return (dintra,
            jnp.broadcast_to(dq[:, :, None], (h, c, RET_DV)),
            jnp.broadcast_to(dk[:, :, None], (h, c, RET_DK)),
            jnp.broadcast_to(dch[:, None, None], (h, 1, RET_DV)))


def _retention(qk, vz, cos, sin, bsz, seq, *, ts=1024, hps=4):
    h, dk, dv, c = RET_HEADS, RET_DK, RET_DV, RET_CHUNK
    ns = seq // ts
    ng = h // hps
    dintra, dq, dkt, dch = _retention_decays()
    row = lambda b, hh, s: b * ns + s
    return pl.pallas_call(
        functools.partial(_retention_kernel, n_sub=ts // c, n_heads=hps),
        grid=(bsz, ng, ns),
        in_specs=[pl.BlockSpec((ts, hps * dk), lambda b, hh, s: (row(b, hh, s), hh)),
                  pl.BlockSpec((ts, hps * dk), lambda b, hh, s: (row(b, hh, s), ng + hh)),
                  pl.BlockSpec((ts, hps * dv), lambda b, hh, s: (row(b, hh, s), hh)),
                  pl.BlockSpec((ts, hps * dv), lambda b, hh, s: (row(b, hh, s), ng + hh)),
                  pl.BlockSpec((ts, dk // 2), lambda b, hh, s: (row(b, hh, s), 0)),
                  pl.BlockSpec((ts, dk // 2), lambda b, hh, s: (row(b, hh, s), 0)),
                  pl.BlockSpec((hps, c, c), lambda b, hh, s: (hh, 0, 0)),
                  pl.BlockSpec((hps, c, dv), lambda b, hh, s: (hh, 0, 0)),
                  pl.BlockSpec((hps, c, dk), lambda b, hh, s: (hh, 0, 0)),
                  pl.BlockSpec((hps, 1, dv), lambda b, hh, s: (hh, 0, 0))],
        out_specs=pl.BlockSpec((ts, hps * dv), lambda b, hh, s: (row(b, hh, s), hh)),
        out_shape=jax.ShapeDtypeStruct((bsz * seq, h * dv), BF16),
        scratch_shapes=[pltpu.VMEM((hps, dk, dv), F32)],
        compiler_params=_params(3),
        name="retention",
    )(qk, qk, vz, vz, cos, sin, dintra, dq, dkt, dch)


def _deinterleave_matrix(width):
    p = np.zeros((width, width), np.float32)
    j = np.arange(width // 2)
    p[2 * j, j] = 1.0
    p[2 * j + 1, width // 2 + j] = 1.0
    return jnp.asarray(p, BF16)


def _proj_rows(xm):
    return 2048 if xm.dtype == BF16 else 1024


def _retention_layer(x, xm, cos, sin, w_in, w_out, gain, bias, bsz, seq):
    h, dk, dv = RET_HEADS, RET_DK, RET_DV
    tn, tm = 1024, _proj_rows(xm)
    qk = _proj(xm, w_in, n_tiles=2 * h * dk // tn, col_block0=0, tm=tm, tn=tn, out_dtype=BF16,
               perm=_deinterleave_matrix(dk))
    vz = _proj(xm, w_in, n_tiles=2 * h * dv // tn, col_block0=2 * h * dk // tn, tm=tm, tn=tn, out_dtype=BF16)
    mixed = _retention(qk, vz, cos, sin, bsz, seq)
    return _out_ln(mixed, w_out.astype(BF16), x, gain, bias)


def _compress_kernel(*refs, transposed):
    x_refs, (pe_ref, w1_ref, w2_ref, o_ref) = refs[:-4], refs[-4:]
    n_blk = o_ref.shape[3] if transposed else o_ref.shape[2]
    half = CMP_LEN // 2
    lo, hi = [], []
    for l in range(half):
        rows = jnp.concatenate([x_ref[pl.ds(l, n_blk, stride=CMP_STRIDE), :] for x_ref in x_refs], axis=1)
        lo.append((rows + pe_ref[l:l + 1, :]).astype(BF16))
        hi.append((rows + pe_ref[half + l:half + l + 1, :]).astype(BF16))
    acc_lo = jnp.dot(jnp.concatenate(lo, axis=1), w1_ref[0], preferred_element_type=F32)
    acc_hi = jnp.dot(jnp.concatenate(hi, axis=1), w1_ref[1], preferred_element_type=F32)
    hid = acc_lo + pltpu.roll(acc_hi, shift=n_blk - 1, axis=0)
    out = jnp.dot(_silu(hid).astype(BF16), w2_ref[...], preferred_element_type=F32)
    keep = lax.broadcasted_iota(jnp.int32, out.shape, 0) < n_blk - 1
    out = jnp.where(keep, out, 0.0)
    if transposed:
        width = out.shape[1]
        o_ref[0, 0, :width, :] = out.T.astype(BF16)
        o_ref[0, 0, width:, :] = jnp.ones((o_ref.shape[2] - width, n_blk), BF16)
    else:
        o_ref[0, 0] = out.astype(BF16)


def _compress(kvc, pe, w1, w2, bsz, seq, *, width, col_block0, out_rows=None):
    g = NSA_GROUPS
    n_blk = seq // CMP_STRIDE
    n_slabs = width // LANES
    out_block = (1, 1, n_blk, w2.shape[1]) if out_rows is None else (1, 1, out_rows, n_blk)
    slab = lambda j: pl.BlockSpec((seq, LANES), lambda b, gg: (b, (col_block0 + gg) * n_slabs + j))
    return pl.pallas_call(
        functools.partial(_compress_kernel, transposed=out_rows is not None),
        grid=(bsz, g),
        in_specs=[slab(j) for j in range(n_slabs)] + [
                  pl.BlockSpec(pe.shape, lambda b, gg: (0, 0)),
                  pl.BlockSpec(w1.shape, lambda b, gg: (0, 0, 0)),
                  pl.BlockSpec(w2.shape, lambda b, gg: (0, 0))],
        out_specs=pl.BlockSpec(out_block, lambda b, gg: (b, gg, 0, 0)),
        out_shape=jax.ShapeDtypeStruct((bsz, g) + out_block[2:], BF16),
        compiler_params=_params(2),
        name="compress",
    )(*([kvc] * n_slabs), pe, w1, w2)


def _values_t_kernel(v_ref, o_ref):
    o_ref[0, 0, :NSA_DV, :] = v_ref[...].astype(F32).T.astype(BF16)
    o_ref[0, 0, NSA_DV:, :] = jnp.ones((V_ROWS - NSA_DV, v_ref.shape[0]), BF16)


def _values_t(kv, bsz, seq):
    g, dv, dkp = NSA_GROUPS, NSA_DV, NSA_DKP
    branch_blocks = g * (dkp + dv) // dv
    v0 = g * dkp // dv
    return pl.pallas_call(
        _values_t_kernel,
        grid=(bsz, 2 * g),
        in_specs=[pl.BlockSpec((seq, dv), lambda b, j: (b, v0 + (j // g) * branch_blocks + j % g))],
        out_specs=pl.BlockSpec((1, 1, V_ROWS, seq), lambda b, j: (b, j, 0, 0)),
        out_shape=jax.ShapeDtypeStruct((bsz, 2 * g, V_ROWS, seq), BF16),
        compiler_params=_params(2),
        name="values_t",
    )(kv)


def _pad_last(a, width):
    return jnp.pad(a, [(0, 0)] * (a.ndim - 1) + [(0, width - a.shape[-1])])


def _nsa_shared_kv(x, w_kv, pe_k, pe_v, w_ck1, w_ck2, w_cv1, w_cv2, bsz, seq):
    g, dk, dv, dkp = NSA_GROUPS, NSA_DK, NSA_DV, NSA_DKP
    tn = g * (dk + dv)
    pad = (g, dk, dkp)
    kvc = _proj(x, w_kv, n_tiles=1, col_block0=0, tm=1024, tn=tn, out_dtype=F32, head_pad=pad)
    kv = _proj(x, w_kv, n_tiles=2, col_block0=1, tm=1024, tn=tn, out_dtype=BF16, head_pad=pad)
    w1k = _pad_last(w_ck1.reshape(CMP_LEN, dk, CMP_HID).transpose(0, 2, 1), dkp).transpose(0, 2, 1).astype(BF16)
    w1k = w1k.reshape(2, CMP_LEN // 2 * dkp, CMP_HID)
    w1v = w_cv1.reshape(2, CMP_LEN // 2 * dv, CMP_HID).astype(BF16)
    kc = _compress(kvc, _pad_last(pe_k, dkp), w1k, _pad_last(w_ck2, dkp).astype(BF16), bsz, seq,
                   width=dkp, col_block0=0)
    vct = _compress(kvc, pe_v, w1v, w_cv2.astype(BF16), bsz, seq, width=dv, col_block0=g * dkp // dv,
                    out_rows=V_ROWS)
    return kc, vct, kv, _values_t(kv, bsz, seq)


def _nsa_attn_kernel(q_ref, zc_ref, zs_ref, zw_ref, x_ref, wg_ref, kc_ref, vct_ref, ovt_ref,
                     ks_ref, vst_ref, kw_ref, vwt_ref, o_ref,
                     bias_ref, sa_ref, sb_ref, m_ref, acc_ref, mix_ref, zsil_ref, wgb_ref, gates_ref):
    tq, r, dv, tk = q_ref.shape[0], NSA_HPG, NSA_DV, SEL_TK
    lanes = r * tq
    n_blk = kc_ref.shape[2]
    n_sel = n_blk * CMP_STRIDE // SEL_LEN
    q0 = pl.program_id(2) * tq
    q4 = jnp.concatenate([q_ref[:, h * NSA_DKP:(h + 1) * NSA_DKP] for h in range(r)], axis=0)

    def qpos(shape):
        return q0 + lax.rem(lax.broadcasted_iota(jnp.int32, shape, 1), tq)

    def all_heads(tile):
        return jnp.concatenate([tile] * r, axis=1)

    def weights(s):
        return jnp.exp2(s - jnp.max(s, axis=0, keepdims=True)).astype(BF16)

    @pl.when((pl.program_id(0) == 0) & (pl.program_id(1) == 0) & (pl.program_id(2) == 0))
    def _():
        real = lax.broadcasted_iota(jnp.int32, wg_ref.shape, 0) < N_BRANCH * NSA_HEADS
        wgb_ref[...] = jnp.where(real, wg_ref[...], 0.0).astype(BF16)

    gates_ref[...] = _sigmoid(lax.dot_general(wgb_ref[...], x_ref[...], NT_DIMS, preferred_element_type=F32))
    head0 = pl.program_id(1) * r

    def gated(o_aug, inv_l, silu_z, br, h):
        cols = slice(h * tq, (h + 1) * tq)
        row_scale = gates_ref[pl.ds(br * NSA_HEADS + head0 + h, 1), :] * inv_l[:, cols]
        return (o_aug[:dv, cols] * row_scale).T * silu_z

    s = lax.dot_general(kc_ref[0, 0], q4, NT_DIMS, preferred_element_type=F32)
    blk_end = lax.broadcasted_iota(jnp.int32, (n_blk, tq), 0) * CMP_STRIDE + (CMP_LEN - 1)
    s = s + all_heads(jnp.where(blk_end <= qpos((n_blk, tq)), 0.0, NEG_INF))
    eb = weights(s)
    o_cmp = jnp.dot(vct_ref[0, 0], eb, preferred_element_type=F32)
    inv_c = jnp.where(qpos((1, lanes)) >= CMP_LEN - 1, 1.0 / o_cmp[dv:dv + 1, :], 0.0)

    imp = sum(jnp.dot(ovt_ref[...], eb[:, h * tq:(h + 1) * tq], preferred_element_type=F32)
              * inv_c[:, h * tq:(h + 1) * tq] for h in range(r))
    imp = imp[:n_sel, :]
    blk = lax.broadcasted_iota(jnp.int32, imp.shape, 0)
    cur = (q0 + lax.broadcasted_iota(jnp.int32, imp.shape, 1)) // SEL_LEN
    forced = (blk == 0) | (blk == cur) | (blk == cur - 1)
    score = jnp.where(forced, FORCE_SCORE, jnp.where(blk <= cur, imp, -1.0))
    groups = [score[g0:g0 + SUBLANES, :] for g0 in range(0, n_sel, SUBLANES)]
    ranks = [jnp.zeros((SUBLANES, tq), F32) for _ in groups]
    for i in range(n_sel):
        row = score[i:i + 1, :]
        for gi, grp in enumerate(groups):
            g0 = gi * SUBLANES
            if g0 > i:
                ahead = row >= grp
            elif g0 + SUBLANES - 1 < i:
                ahead = row > grp
            else:
                after = lax.broadcasted_iota(jnp.int32, grp.shape, 0) + g0 > i
                ahead = (row > grp) | (after & (row == grp))
            ranks[gi] = ranks[gi] + jnp.where(ahead, 1.0, 0.0)
    rank = jnp.concatenate(ranks, axis=0)
    bias_ref[...] = jnp.where(rank < float(min(SEL_TOPK, n_sel)), 0.0, NEG_INF)

    span = WINDOW + tq
    k0 = pl.multiple_of(jnp.maximum(q0 - WINDOW, 0), tq)
    s = lax.dot_general(kw_ref[pl.ds(k0, span), :], q4, NT_DIMS, preferred_element_type=F32)
    dist = qpos((span, tq)) - (k0 + lax.broadcasted_iota(jnp.int32, (span, tq), 0))
    s = s + all_heads(jnp.where((dist >= 0) & (dist < WINDOW), 0.0, NEG_INF))
    o_win = jnp.dot(vwt_ref[0, 0, :, pl.ds(k0, span)], weights(s), preferred_element_type=F32)
    inv_w = 1.0 / o_win[dv:dv + 1, :]

    for h in range(r):
        cols = slice(h * dv, (h + 1) * dv)
        mix_ref[:, cols] = (gated(o_cmp, inv_c, _silu(zc_ref[:, cols].astype(F32)), 0, h)
                            + gated(o_win, inv_w, _silu(zw_ref[:, cols].astype(F32)), 2, h))
    zsil_ref[...] = _silu(zs_ref[...].astype(F32))

    m_ref[...] = jnp.full(m_ref.shape, NEG_INF, F32)
    acc_ref[...] = jnp.zeros_like(acc_ref)
    blocks_per_tile = tk // SEL_LEN
    n_tiles = q0 // tk + 1
    causal_gap = (q0 + lax.broadcasted_iota(jnp.int32, (tk, tq), 1)) - lax.broadcasted_iota(jnp.int32, (tk, tq), 0)

    def scores(tile):
        kbase = pl.multiple_of(tile * tk, tk)
        rows = bias_ref[pl.ds(pl.multiple_of(tile * blocks_per_tile, blocks_per_tile), blocks_per_tile), :]
        bias = jnp.concatenate([jnp.broadcast_to(rows[j:j + 1, :], (SEL_LEN, tq)) for j in range(blocks_per_tile)],
                               axis=0)
        bias = jnp.where(causal_gap >= kbase, bias, NEG_INF)
        return (lax.dot_general(ks_ref[pl.ds(kbase, tk), :], q4, NT_DIMS, preferred_element_type=F32)
                + all_heads(bias))

    def attend(s_ref, tile):
        kbase = pl.multiple_of(tile * tk, tk)
        s = s_ref[...]
        m_prev = m_ref[...]
        m_new = jnp.maximum(m_prev, jnp.max(s, axis=0, keepdims=True))
        pv = jnp.dot(vst_ref[0, 0, :, pl.ds(kbase, tk)], jnp.exp2(s - m_new).astype(BF16),
                     preferred_element_type=F32)
        acc_ref[...] = jnp.exp2(m_prev - m_new) * acc_ref[...] + pv
        m_ref[...] = m_new

    sa_ref[...] = scores(0)

    def pair(jp, carry):
        first = 2 * jp
        sb_ref[...] = scores(first + 1)
        attend(sa_ref, first)
        sa_ref[...] = scores(jnp.minimum(first + 2, n_tiles - 1))
        attend(sb_ref, first + 1)
        return carry

    lax.fori_loop(0, n_tiles // 2, pair, 0)

    @pl.when(lax.rem(n_tiles, 2) == 1)
    def _():
        attend(sa_ref, n_tiles - 1)

    o_sel = acc_ref[...]
    inv_s = 1.0 / o_sel[dv:dv + 1, :]
    for h in range(r):
        cols = slice(h * dv, (h + 1) * dv)
        o_ref[:, cols] = (mix_ref[:, cols] + gated(o_sel, inv_s, zsil_ref[:, cols], 1, h)).astype(BF16)


def _overlap_matrix_t(n_blk, n_sel):
    a, b = SEL_LEN // CMP_STRIDE, CMP_LEN // CMP_STRIDE
    span = a + b - 2
    n = np.arange(n_blk)[None, :]
    j = np.arange(LANES)[:, None]
    diff = n - a * j
    ov = np.where((diff >= 0) & (diff <= span), np.minimum(np.minimum(diff, span - diff), min(a, b) - 1) + 1, 0)
    ov = np.where((n < n_blk - 1) & (j < n_sel), ov, 0)
    return jnp.asarray(ov, BF16)


def _nsa_attention(qp, zp, xm, w_t, gate_block, kc, vct, kv, vt, bsz, seq):
    g, r, dv, dkp, tq, tk = NSA_GROUPS, NSA_HPG, NSA_DV, NSA_DKP, ATT_TQ, SEL_TK
    d = xm.shape[1]
    nq = seq // tq
    n_blk = seq // CMP_STRIDE
    n_sel = seq // SEL_LEN
    ovt = _overlap_matrix_t(n_blk, n_sel)
    row = lambda b, gg, i: b * nq + i
    z_spec = lambda br: pl.BlockSpec((tq, r * dv), lambda b, gg, i: (row(b, gg, i), br * g + gg))
    ks0, kw0 = 0, g * (dkp + dv) // dkp
    return pl.pallas_call(
        _nsa_attn_kernel,
        grid=(bsz, g, nq),
        in_specs=[pl.BlockSpec((tq, r * dkp), lambda b, gg, i: (row(b, gg, i), gg)),
                  z_spec(0), z_spec(1), z_spec(2),
                  pl.BlockSpec((tq, d), lambda b, gg, i: (row(b, gg, i), 0)),
                  pl.BlockSpec((LANES, d), lambda b, gg, i: (gate_block, 0)),
                  pl.BlockSpec((1, 1, n_blk, dkp), lambda b, gg, i: (b, gg, 0, 0)),
                  pl.BlockSpec((1, 1, V_ROWS, n_blk), lambda b, gg, i: (b, gg, 0, 0)),
                  pl.BlockSpec(ovt.shape, lambda b, gg, i: (0, 0)),
                  pl.BlockSpec((seq, dkp), lambda b, gg, i: (b, ks0 + gg)),
                  pl.BlockSpec((1, 1, V_ROWS, seq), lambda b, gg, i: (b, gg, 0, 0)),
                  pl.BlockSpec((seq, dkp), lambda b, gg, i: (b, kw0 + gg)),
                  pl.BlockSpec((1, 1, V_ROWS, seq), lambda b, gg, i: (b, g + gg, 0, 0))],
        out_specs=pl.BlockSpec((tq, r * dv), lambda b, gg, i: (row(b, gg, i), gg)),
        out_shape=jax.ShapeDtypeStruct((bsz * seq, g * r * dv), BF16),
        scratch_shapes=[pltpu.VMEM((n_sel, tq), F32),
                        pltpu.VMEM((tk, r * tq), F32),
                        pltpu.VMEM((tk, r * tq), F32),
                        pltpu.VMEM((1, r * tq), F32),
                        pltpu.VMEM((V_ROWS, r * tq), F32),
                        pltpu.VMEM((tq, r * dv), F32),
                        pltpu.VMEM((tq, r * dv), F32),
                        pltpu.VMEM((LANES, d), BF16),
                        pltpu.VMEM((LANES, tq), F32)],
        compiler_params=_params(3),
        name="nsa_attn",
    )(qp, zp, zp, zp, xm, w_t, kc, vct, ovt, kv, vt, kv, vt)


def _nsa_layer(x, xm, shared, w_in, w_out, gain, bias, bsz, seq, *, last):
    kc, vct, kv, vt = shared
    h, g, r, dk, dv, dkp = NSA_HEADS, NSA_GROUPS, NSA_HPG, NSA_DK, NSA_DV, NSA_DKP
    nq = h * dk
    nz = N_BRANCH * h * dv
    tn, tm = 1024, _proj_rows(xm)
    w_t = w_in.T
    qp = _proj(xm, w_t, n_tiles=g, col_block0=0, tm=tm, tn=r * dk, out_dtype=BF16, w_transposed=True,
               head_pad=(r, dk, dkp), out_scale=(dk ** -0.5) * math.log2(math.e))
    zp = _proj(xm, w_t, n_tiles=nz // tn, col_block0=nq // tn, tm=tm, tn=tn, out_dtype=BF16, w_transposed=True)
    mixed = _nsa_attention(qp, zp, xm, w_t, (nq + nz) // LANES, kc, vct, kv, vt, bsz, seq)
    return _out_ln(mixed, w_out.astype(BF16), x, gain, bias, with_bf16=not last)


def kernel(x, positions, ret_w_in_0, ret_w_out_0, ln_g_0, ln_b_0, ret_w_in_1, ret_w_out_1, ln_g_1, ln_b_1,
           nsa_w_kv, nsa_pe_k, nsa_pe_v, nsa_w_ck1, nsa_w_ck2, nsa_w_cv1, nsa_w_cv2,
           nsa_w_in_2, nsa_w_out_2, ln_g_2, ln_b_2, nsa_w_in_3, nsa_w_out_3, ln_g_3, ln_b_3):
    bsz, seq, d = x.shape
    h = x.reshape(bsz * seq, d)
    cos, sin = _rope_tables(positions)
    h, hm = _retention_layer(h, h, cos, sin, ret_w_in_0, ret_w_out_0, ln_g_0, ln_b_0, bsz, seq)
    h, hm = _retention_layer(h, hm, cos, sin, ret_w_in_1, ret_w_out_1, ln_g_1, ln_b_1, bsz, seq)
    shared = _nsa_shared_kv(hm, nsa_w_kv, nsa_pe_k, nsa_pe_v, nsa_w_ck1, nsa_w_ck2, nsa_w_cv1, nsa_w_cv2, bsz, seq)
    h, hm = _nsa_layer(h, hm, shared, nsa_w_in_2, nsa_w_out_2, ln_g_2, ln_b_2, bsz, seq, last=False)
    h = _nsa_layer(h, hm, shared, nsa_w_in_3, nsa_w_out_3, ln_g_3, ln_b_3, bsz, seq, last=True)
    return h.reshape(bsz, seq, d)
```

```python
import functools
import math

import numpy as np
import jax
import jax.numpy as jnp
from jax import lax
from jax.experimental import pallas as pl
from jax.experimental.pallas import tpu as pltpu

D_MODEL = 2048
DEPTH = 4
N_A_LAYERS = DEPTH // 2
RET_HEADS = 8
RET_DK = D_MODEL // RET_HEADS
RET_DV = 2 * D_MODEL // RET_HEADS
RET_CHUNK = 128
RET_THETA_BASE = 10000.0
NSA_HEADS = 16
NSA_GROUPS = 4
NSA_HPG = NSA_HEADS // NSA_GROUPS
NSA_DV = D_MODEL // NSA_HEADS
NSA_DK = 3 * NSA_DV // 2
NSA_DKP = 256
CMP_LEN = 32
CMP_STRIDE = 16
CMP_HID = 2 * NSA_DV
SEL_LEN = 64
SEL_TOPK = 16
WINDOW = 512
N_BRANCH = 3
ALPHA = (2.0 * DEPTH) ** 0.25
NEG_INF = -1e30
FORCE_SCORE = 1e9
LN_EPS = 1e-5

LANES = 128
SUBLANES = 8
BF16_SUBLANES = 16
ATT_TQ = 256
SEL_TK = 512
V_ROWS = NSA_DV + BF16_SUBLANES
VMEM_LIMIT = 56 * 1024 * 1024

F32 = jnp.float32
BF16 = jnp.bfloat16
NT_DIMS = (((1,), (1,)), ((), ()))
TN_DIMS = (((0,), (0,)), ((), ()))


def _params(n_axes):
    return pltpu.CompilerParams(dimension_semantics=("arbitrary",) * n_axes, vmem_limit_bytes=VMEM_LIMIT)


def _sigmoid(x):
    return 0.5 * jnp.tanh(0.5 * x) + 0.5


def _silu(x):
    half = 0.5 * x
    return half + half * jnp.tanh(half)


def _proj_kernel(x_ref, w_ref, *rest, out_scale, w_transposed, head_pad, has_perm, has_rope, has_decay, silu_out):
    rest = list(rest)
    perm_ref = rest.pop(0) if has_perm else None
    cos_ref, sin_ref = (rest.pop(0), rest.pop(0)) if has_rope else (None, None)
    decay_ref = rest.pop(0) if has_decay else None
    o_ref = rest.pop(0)
    od_ref = rest.pop(0) if has_decay else None
    (wbf_ref,) = rest

    @pl.when(pl.program_id(1) == 0)
    def _():
        if perm_ref is None:
            wbf_ref[...] = w_ref[...].astype(BF16)
        else:
            blk = perm_ref.shape[0]
            for c in range(w_ref.shape[1] // blk):
                cols = slice(c * blk, (c + 1) * blk)
                wbf_ref[:, cols] = jnp.dot(w_ref[:, cols].astype(BF16), perm_ref[...],
                                           preferred_element_type=F32).astype(BF16)

    xb = x_ref[...]
    if xb.dtype != BF16:
        xb = xb.astype(BF16)
    if w_transposed:
        acc = lax.dot_general(xb, wbf_ref[...], NT_DIMS, preferred_element_type=F32)
    else:
        acc = jnp.dot(xb, wbf_ref[...], preferred_element_type=F32)
    if out_scale is not None:
        acc = acc * out_scale
    if head_pad is not None:
        n_heads, width, padded = head_pad
        pieces = []
        for h in range(n_heads):
            pieces += [acc[:, h * width:(h + 1) * width], jnp.zeros((acc.shape[0], padded - width), F32)]
        if acc.shape[1] > n_heads * width:
            pieces.append(acc[:, n_heads * width:])
        acc = jnp.concatenate(pieces, axis=1)
    if silu_out:
        acc = _silu(acc)
    if has_rope:
        cosv, sinv = cos_ref[...], sin_ref[...]
        half = cosv.shape[1]
        pieces = []
        for h in range(acc.shape[1] // (2 * half)):
            even, odd = acc[:, 2 * h * half:(2 * h + 1) * half], acc[:, (2 * h + 1) * half:(2 * h + 2) * half]
            pieces += [even * cosv - odd * sinv, even * sinv + odd * cosv]
        acc = jnp.concatenate(pieces, axis=1)
    o_ref[...] = acc.astype(o_ref.dtype)
    if has_decay:
        n_heads, chunk, width = decay_ref.shape
        for h in range(n_heads):
            cols = slice(h * width, (h + 1) * width)
            for r0 in range(0, acc.shape[0], chunk):
                od_ref[r0:r0 + chunk, cols] = (acc[r0:r0 + chunk, cols] * decay_ref[h]).astype(od_ref.dtype)


def _proj(x, w, *, n_tiles, col_block0, tm, tn, out_dtype, perm=None, out_scale=None, w_transposed=False,
          head_pad=None, silu_out=False, rope=None, decay=None):
    m, k = x.shape
    tn_out = tn if head_pad is None else tn + head_pad[0] * (head_pad[2] - head_pad[1])
    if w_transposed:
        w_spec = pl.BlockSpec((tn, k), lambda j, i: (j + col_block0, 0))
    else:
        w_spec = pl.BlockSpec((k, tn), lambda j, i: (0, j + col_block0))
    in_specs = [pl.BlockSpec((tm, k), lambda j, i: (i, 0)), w_spec]
    args = [x, w]
    if perm is not None:
        in_specs.append(pl.BlockSpec(perm.shape, lambda j, i: (0, 0)))
        args.append(perm)
    if rope is not None:
        in_specs += [pl.BlockSpec((tm, rope[0].shape[1]), lambda j, i: (i, 0))] * 2
        args += list(rope)
    o_spec = pl.BlockSpec((tm, tn_out), lambda j, i: (i, j))
    o_shape = jax.ShapeDtypeStruct((m, n_tiles * tn_out), out_dtype)
    if decay is not None:
        heads_per_tile = tn // decay.shape[2]
        in_specs.append(pl.BlockSpec((heads_per_tile,) + decay.shape[1:], lambda j, i: (j, 0, 0)))
        args.append(decay)
    return pl.pallas_call(
        functools.partial(_proj_kernel, out_scale=out_scale, w_transposed=w_transposed, head_pad=head_pad,
                          has_perm=perm is not None, has_rope=rope is not None, has_decay=decay is not None,
                          silu_out=silu_out),
        grid=(n_tiles, m // tm),
        in_specs=in_specs,
        out_specs=[o_spec, o_spec] if decay is not None else o_spec,
        out_shape=[o_shape, o_shape] if decay is not None else o_shape,
        scratch_shapes=[pltpu.VMEM((tn, k) if w_transposed else (k, tn), BF16)],
        compiler_params=_params(2),
        name="proj",
    )(*args)


OUT_LN_SPLIT = 4


def _out_ln_kernel(a_ref, w_ref, x_ref, g_ref, b_ref, o_ref, *o16_ref):
    rows = a_ref.shape[0] // OUT_LN_SPLIT
    for part in range(OUT_LN_SPLIT):
        sl = slice(part * rows, (part + 1) * rows)
        u = ALPHA * x_ref[sl, :] + jnp.dot(a_ref[sl, :], w_ref[...], preferred_element_type=F32)
        mu = jnp.mean(u, axis=-1, keepdims=True)
        var = jnp.mean(jnp.square(u - mu), axis=-1, keepdims=True)
        y = (u - mu) * lax.rsqrt(var + LN_EPS) * g_ref[...] + b_ref[...]
        o_ref[sl, :] = y
        if o16_ref:
            o16_ref[0][sl, :] = y.astype(BF16)


def _out_ln(a, w, x, gain, bias, *, tm=512, with_bf16=True):
    m, ka = a.shape
    d = w.shape[1]
    o_spec = pl.BlockSpec((tm, d), lambda i: (i, 0))
    return pl.pallas_call(
        _out_ln_kernel,
        grid=(m // tm,),
        in_specs=[pl.BlockSpec((tm, ka), lambda i: (i, 0)),
                  pl.BlockSpec((ka, d), lambda i: (0, 0), pipeline_mode=pl.Buffered(1)),
                  pl.BlockSpec((tm, d), lambda i: (i, 0)),
                  pl.BlockSpec((1, d), lambda i: (0, 0)),
                  pl.BlockSpec((1, d), lambda i: (0, 0))],
        out_specs=[o_spec, o_spec] if with_bf16 else o_spec,
        out_shape=([jax.ShapeDtypeStruct((m, d), F32), jax.ShapeDtypeStruct((m, d), BF16)] if with_bf16
                   else jax.ShapeDtypeStruct((m, d), F32)),
        compiler_params=_params(1),
        name="out_ln",
    )(a, w, x, gain.reshape(1, d), bias.reshape(1, d))


def _rope_kernel(pos_ref, freq_ref, cos_ref, sin_ref):
    ang = pos_ref[...].astype(F32) * freq_ref[...]
    cos_ref[...] = jnp.cos(ang)
    sin_ref[...] = jnp.sin(ang)


def _rope_tables(positions, tm=1024):
    t = positions.size
    half = RET_DK // 2
    inv_freq = 1.0 / (RET_THETA_BASE ** jnp.linspace(0.0, 1.0, half, dtype=F32))
    return pl.pallas_call(
        _rope_kernel,
        grid=(t // tm,),
        in_specs=[pl.BlockSpec((tm, 1), lambda i: (i, 0)), pl.BlockSpec((1, half), lambda i: (0, 0))],
        out_specs=[pl.BlockSpec((tm, half), lambda i: (i, 0))] * 2,
        out_shape=[jax.ShapeDtypeStruct((t, half), F32)] * 2,
        compiler_params=_params(1),
        name="rope_tables",
    )(positions.reshape(t, 1), inv_freq.reshape(1, half))


def _retention_kernel(q_ref, k_ref, kd_ref, v_ref, zs_ref, dintra_ref, dq_ref, dch_ref, o_ref, state_ref,
                      *, n_sub, n_heads):
    c = RET_CHUNK

    @pl.when(pl.program_id(2) == 0)
    def _():
        state_ref[...] = jnp.zeros_like(state_ref)

    for j in range(n_sub):
        rows = pl.ds(j * c, c)
        for hh in range(n_heads):
            kcols = slice(hh * RET_DK, (hh + 1) * RET_DK)
            vcols = slice(hh * RET_DV, (hh + 1) * RET_DV)
            qb = q_ref[rows, kcols]
            v = v_ref[rows, vcols]
            state = state_ref[hh]
            scores = lax.dot_general(qb, k_ref[rows, kcols], NT_DIMS, preferred_element_type=F32) * dintra_ref[hh]
            out = (jnp.dot(scores.astype(BF16), v, preferred_element_type=F32)
                   + jnp.dot(qb, state.astype(BF16), preferred_element_type=F32) * dq_ref[hh])
            state_ref[hh] = state * dch_ref[hh] + lax.dot_general(kd_ref[rows, kcols], v, TN_DIMS,
                                                                  preferred_element_type=F32)
            mu = jnp.mean(out, axis=-1, keepdims=True)
            var = jnp.mean(jnp.square(out - mu), axis=-1, keepdims=True)
            normed = (out - mu) * lax.rsqrt(var + LN_EPS)
            o_ref[rows, vcols] = (normed * zs_ref[rows, vcols].astype(F32)).astype(BF16)


def _retention_decays():
    h, c = RET_HEADS, RET_CHUNK
    k_scale = RET_DK ** -0.5
    assert math.frexp(k_scale)[0] == 0.5
    log_gamma = jnp.log1p(-jnp.exp2(-5.0 - jnp.arange(h, dtype=F32)))
    idx = jnp.arange(c, dtype=F32)
    rel = idx[:, None] - idx[None, :]
    dintra = jnp.where(rel >= 0, jnp.exp(log_gamma[:, None, None] * jnp.maximum(rel, 0.0)), 0.0) * k_scale
    dq = jnp.exp(log_gamma[:, None] * (idx + 1.0))
    dk = jnp.exp(log_gamma[:, None] * (c - 1.0 - idx)) * k_scale
    dch = jnp.exp(log_gamma * c)
    return (dintra,
            jnp.broadcast_to(dq[:, :, None], (h, c, RET_DV)),
            jnp.broadcast_to(dk[:, :, None], (h, c, RET_DK)),
            jnp.broadcast_to(dch[:, None, None], (h, 1, RET_DV)))


def _retention(q, k, kd, v, zs, tables, bsz, seq, *, ts=1024, hps=4):
    h, dk, dv, c = RET_HEADS, RET_DK, RET_DV, RET_CHUNK
    ns = seq // ts
    dintra, dq, _, dch = tables
    k_spec = pl.BlockSpec((ts, hps * dk), lambda b, hh, s: (b * ns + s, hh))
    v_spec = pl.BlockSpec((ts, hps * dv), lambda b, hh, s: (b * ns + s, hh))
    return pl.pallas_call(
        functools.partial(_retention_kernel, n_sub=ts // c, n_heads=hps),
        grid=(bsz, h // hps, ns),
        in_specs=[k_spec, k_spec, k_spec, v_spec, v_spec,
                  pl.BlockSpec((hps, c, c), lambda b, hh, s: (hh, 0, 0)),
                  pl.BlockSpec((hps, c, dv), lambda b, hh, s: (hh, 0, 0)),
                  pl.BlockSpec((hps, 1, dv), lambda b, hh, s: (hh, 0, 0))],
        out_specs=v_spec,
        out_shape=jax.ShapeDtypeStruct((bsz * seq, h * dv), BF16),
        scratch_shapes=[pltpu.VMEM((hps, dk, dv), F32)],
        compiler_params=_params(3),
        name="retention",
    )(q, k, kd, v, zs, dintra, dq, dch)


def _deinterleave_matrix(width):
    p = np.zeros((width, width), np.float32)
    j = np.arange(width // 2)
    p[2 * j, j] = 1.0
    p[2 * j + 1, width // 2 + j] = 1.0
    return jnp.asarray(p, BF16)


def _proj_rows(xm):
    return 2048 if xm.dtype == BF16 else 1024


def _retention_layer(x, xm, cos, sin, w_in, w_out, gain, bias, bsz, seq):
    h, dk, dv = RET_HEADS, RET_DK, RET_DV
    tn, tm = 1024, _proj_rows(xm)
    tables = _retention_decays()
    perm = _deinterleave_matrix(dk)
    nk, nv = h * dk // tn, h * dv // tn
    tm_qk = 1024
    q = _proj(xm, w_in, n_tiles=nk, col_block0=0, tm=tm_qk, tn=tn, out_dtype=BF16, perm=perm, rope=(cos, sin))
    k, kd = _proj(xm, w_in, n_tiles=nk, col_block0=nk, tm=tm_qk, tn=tn, out_dtype=BF16, perm=perm, rope=(cos, sin),
                  decay=tables[2])
    v = _proj(xm, w_in, n_tiles=nv, col_block0=2 * nk, tm=tm, tn=tn, out_dtype=BF16)
    zs = _proj(xm, w_in, n_tiles=nv, col_block0=2 * nk + nv, tm=tm, tn=tn, out_dtype=BF16, silu_out=True)
    mixed = _retention(q, k, kd, v, zs, tables, bsz, seq)
    return _out_ln(mixed, w_out.astype(BF16), x, gain, bias)


def _compress_kernel(*refs, transposed):
    x_refs, (pe_ref, w1_ref, w2_ref, o_ref) = refs[:-4], refs[-4:]
    n_blk = o_ref.shape[3] if transposed else o_ref.shape[2]
    half = CMP_LEN // 2
    lo, hi = [], []
    for l in range(half):
        rows = jnp.concatenate([x_ref[pl.ds(l, n_blk, stride=CMP_STRIDE), :] for x_ref in x_refs], axis=1)
        lo.append((rows + pe_ref[l:l + 1, :]).astype(BF16))
        hi.append((rows + pe_ref[half + l:half + l + 1, :]).astype(BF16))
    acc_lo = jnp.dot(jnp.concatenate(lo, axis=1), w1_ref[0], preferred_element_type=F32)
    acc_hi = jnp.dot(jnp.concatenate(hi, axis=1), w1_ref[1], preferred_element_type=F32)
    hid = acc_lo + pltpu.roll(acc_hi, shift=n_blk - 1, axis=0)
    out = jnp.dot(_silu(hid).astype(BF16), w2_ref[...], preferred_element_type=F32)
    keep = lax.broadcasted_iota(jnp.int32, out.shape, 0) < n_blk - 1
    out = jnp.where(keep, out, 0.0)
    if transposed:
        width = out.shape[1]
        o_ref[0, 0, :width, :] = out.T.astype(BF16)
        o_ref[0, 0, width:, :] = jnp.ones((o_ref.shape[2] - width, n_blk), BF16)
    else:
        o_ref[0, 0] = out.astype(BF16)


def _compress(kvc, pe, w1, w2, bsz, seq, *, width, col_block0, out_rows=None):
    g = NSA_GROUPS
    n_blk = seq // CMP_STRIDE
    n_slabs = width // LANES
    out_block = (1, 1, n_blk, w2.shape[1]) if out_rows is None else (1, 1, out_rows, n_blk)
    slab = lambda j: pl.BlockSpec((seq, LANES), lambda b, gg: (b, (col_block0 + gg) * n_slabs + j))
    return pl.pallas_call(
        functools.partial(_compress_kernel, transposed=out_rows is not None),
        grid=(bsz, g),
        in_specs=[slab(j) for j in range(n_slabs)] + [
                  pl.BlockSpec(pe.shape, lambda b, gg: (0, 0)),
                  pl.BlockSpec(w1.shape, lambda b, gg: (0, 0, 0)),
                  pl.BlockSpec(w2.shape, lambda b, gg: (0, 0))],
        out_specs=pl.BlockSpec(out_block, lambda b, gg: (b, gg, 0, 0)),
        out_shape=jax.ShapeDtypeStruct((bsz, g) + out_block[2:], BF16),
        compiler_params=_params(2),
        name="compress",
    )(*([kvc] * n_slabs), pe, w1, w2)


def _values_t_kernel(v_ref, o_ref):
    o_ref[0, 0, :NSA_DV, :] = v_ref[...].astype(F32).T.astype(BF16)
    o_ref[0, 0, NSA_DV:, :] = jnp.ones((V_ROWS - NSA_DV, v_ref.shape[0]), BF16)


def _values_t(kv, bsz, seq):
    g, dv, dkp = NSA_GROUPS, NSA_DV, NSA_DKP
    branch_blocks = g * (dkp + dv) // dv
    v0 = g * dkp // dv
    return pl.pallas_call(
        _values_t_kernel,
        grid=(bsz, 2 * g),
        in_specs=[pl.BlockSpec((seq, dv), lambda b, j: (b, v0 + (j // g) * branch_blocks + j % g))],
        out_specs=pl.BlockSpec((1, 1, V_ROWS, seq), lambda b, j: (b, j, 0, 0)),
        out_shape=jax.ShapeDtypeStruct((bsz, 2 * g, V_ROWS, seq), BF16),
        compiler_params=_params(2),
        name="values_t",
    )(kv)


def _pad_last(a, width):
    return jnp.pad(a, [(0, 0)] * (a.ndim - 1) + [(0, width - a.shape[-1])])


def _nsa_shared_kv(x, w_kv, pe_k, pe_v, w_ck1, w_ck2, w_cv1, w_cv2, bsz, seq):
    g, dk, dv, dkp = NSA_GROUPS, NSA_DK, NSA_DV, NSA_DKP
    tn = g * (dk + dv)
    pad = (g, dk, dkp)
    kvc = _proj(x, w_kv, n_tiles=1, col_block0=0, tm=1024, tn=tn, out_dtype=F32, head_pad=pad)
    kv = _proj(x, w_kv, n_tiles=2, col_block0=1, tm=1024, tn=tn, out_dtype=BF16, head_pad=pad)
    w1k = _pad_last(w_ck1.reshape(CMP_LEN, dk, CMP_HID).transpose(0, 2, 1), dkp).transpose(0, 2, 1).astype(BF16)
    w1k = w1k.reshape(2, CMP_LEN // 2 * dkp, CMP_HID)
    w1v = w_cv1.reshape(2, CMP_LEN // 2 * dv, CMP_HID).astype(BF16)
    kc = _compress(kvc, _pad_last(pe_k, dkp), w1k, _pad_last(w_ck2, dkp).astype(BF16), bsz, seq,
                   width=dkp, col_block0=0)
    vct = _compress(kvc, pe_v, w1v, w_cv2.astype(BF16), bsz, seq, width=dv, col_block0=g * dkp // dv,
                    out_rows=V_ROWS)
    return kc, vct, kv, _values_t(kv, bsz, seq)


def _nsa_attn_kernel(q_ref, zc_ref, zs_ref, zw_ref, x_ref, wg_ref, kc_ref, vct_ref, ovt_ref,
                     ks_ref, vst_ref, kw_ref, vwt_ref, o_ref,
                     bias_ref, sa_ref, sb_ref, m_ref, acc_ref, mix_ref, wgb_ref, gates_ref):
    tq, r, dv, tk = q_ref.shape[0], NSA_HPG, NSA_DV, SEL_TK
    lanes = r * tq
    n_blk = kc_ref.shape[2]
    n_sel = n_blk * CMP_STRIDE // SEL_LEN
    q0 = pl.program_id(2) * tq
    q4 = jnp.concatenate([q_ref[:, h * NSA_DKP:(h + 1) * NSA_DKP] for h in range(r)], axis=0)

    def qpos(shape):
        return q0 + lax.rem(lax.broadcasted_iota(jnp.int32, shape, 1), tq)

    def all_heads(tile):
        return jnp.concatenate([tile] * r, axis=1)

    def weights(s):
        return jnp.exp2(s - jnp.max(s, axis=0, keepdims=True)).astype(BF16)

    @pl.when((pl.program_id(0) == 0) & (pl.program_id(1) == 0) & (pl.program_id(2) == 0))
    def _():
        real = lax.broadcasted_iota(jnp.int32, wg_ref.shape, 0) < N_BRANCH * NSA_HEADS
        wgb_ref[...] = jnp.where(real, wg_ref[...], 0.0).astype(BF16)

    gates_ref[...] = _sigmoid(lax.dot_general(wgb_ref[...], x_ref[...], NT_DIMS, preferred_element_type=F32))
    head0 = pl.program_id(1) * r

    def gated(o_aug, inv_l, silu_z, br, h):
        cols = slice(h * tq, (h + 1) * tq)
        row_scale = gates_ref[pl.ds(br * NSA_HEADS + head0 + h, 1), :] * inv_l[:, cols]
        return (o_aug[:dv, cols] * row_scale).T * silu_z

    s = lax.dot_general(kc_ref[0, 0], q4, NT_DIMS, preferred_element_type=F32)
    blk_end = lax.broadcasted_iota(jnp.int32, (n_blk, tq), 0) * CMP_STRIDE + (CMP_LEN - 1)
    s = s + all_heads(jnp.where(blk_end <= qpos((n_blk, tq)), 0.0, NEG_INF))
    eb = weights(s)
    o_cmp = jnp.dot(vct_ref[0, 0], eb, preferred_element_type=F32)
    inv_c = jnp.where(qpos((1, lanes)) >= CMP_LEN - 1, 1.0 / o_cmp[dv:dv + 1, :], 0.0)

    imp = sum(jnp.dot(ovt_ref[...], eb[:, h * tq:(h + 1) * tq], preferred_element_type=F32)
              * inv_c[:, h * tq:(h + 1) * tq] for h in range(r))
    imp = imp[:n_sel, :]
    blk = lax.broadcasted_iota(jnp.int32, imp.shape, 0)
    cur = (q0 + lax.broadcasted_iota(jnp.int32, imp.shape, 1)) // SEL_LEN
    forced = (blk == 0) | (blk == cur) | (blk == cur - 1)
    score = jnp.where(forced, FORCE_SCORE, jnp.where(blk <= cur, imp, -1.0))
    groups = [score[g0:g0 + SUBLANES, :] for g0 in range(0, n_sel, SUBLANES)]
    ranks = [jnp.zeros((SUBLANES, tq), F32) for _ in groups]
    for i in range(n_sel):
        row = score[i:i + 1, :]
        for gi, grp in enumerate(groups):
            g0 = gi * SUBLANES
            if g0 > i:
                ahead = row >= grp
            elif g0 + SUBLANES - 1 < i:
                ahead = row > grp
            else:
                after = lax.broadcasted_iota(jnp.int32, grp.shape, 0) + g0 > i
                ahead = (row > grp) | (after & (row == grp))
            ranks[gi] = ranks[gi] + jnp.where(ahead, 1.0, 0.0)
    rank = jnp.concatenate(ranks, axis=0)
    bias_ref[...] = jnp.where(rank < float(min(SEL_TOPK, n_sel)), 0.0, NEG_INF)

    span = WINDOW + tq
    k0 = pl.multiple_of(jnp.maximum(q0 - WINDOW, 0), tq)
    s = lax.dot_general(kw_ref[pl.ds(k0, span), :], q4, NT_DIMS, preferred_element_type=F32)
    dist = qpos((span, tq)) - (k0 + lax.broadcasted_iota(jnp.int32, (span, tq), 0))
    s = s + all_heads(jnp.where((dist >= 0) & (dist < WINDOW), 0.0, NEG_INF))
    o_win = jnp.dot(vwt_ref[0, 0, :, pl.ds(k0, span)], weights(s), preferred_element_type=F32)
    inv_w = 1.0 / o_win[dv:dv + 1, :]

    for h in range(r):
        cols = slice(h * dv, (h + 1) * dv)
        mix_ref[:, cols] = (gated(o_cmp, inv_c, zc_ref[:, cols].astype(F32), 0, h)
                            + gated(o_win, inv_w, zw_ref[:, cols].astype(F32), 2, h))

    m_ref[...] = jnp.full(m_ref.shape, NEG_INF, F32)
    acc_ref[...] = jnp.zeros_like(acc_ref)
    blocks_per_tile = tk // SEL_LEN
    n_tiles = q0 // tk + 1
    causal_gap = (q0 + lax.broadcasted_iota(jnp.int32, (tk, tq), 1)) - lax.broadcasted_iota(jnp.int32, (tk, tq), 0)

    def scores(tile):
        kbase = pl.multiple_of(tile * tk, tk)
        rows = bias_ref[pl.ds(pl.multiple_of(tile * blocks_per_tile, blocks_per_tile), blocks_per_tile), :]
        bias = jnp.concatenate([jnp.broadcast_to(rows[j:j + 1, :], (SEL_LEN, tq)) for j in range(blocks_per_tile)],
                               axis=0)
        bias = jnp.where(causal_gap >= kbase, bias, NEG_INF)
        return (lax.dot_general(ks_ref[pl.ds(kbase, tk), :], q4, NT_DIMS, preferred_element_type=F32)
                + all_heads(bias))

    def attend(s_ref, tile):
        kbase = pl.multiple_of(tile * tk, tk)
        s = s_ref[...]
        m_prev = m_ref[...]
        m_new = jnp.maximum(m_prev, jnp.max(s, axis=0, keepdims=True))
        pv = jnp.dot(vst_ref[0, 0, :, pl.ds(kbase, tk)], jnp.exp2(s - m_new).astype(BF16),
                     preferred_element_type=F32)
        acc_ref[...] = jnp.exp2(m_prev - m_new) * acc_ref[...] + pv
        m_ref[...] = m_new

    sa_ref[...] = scores(0)

    def pair(jp, carry):
        first = 2 * jp
        sb_ref[...] = scores(first + 1)
        attend(sa_ref, first)
        sa_ref[...] = scores(jnp.minimum(first + 2, n_tiles - 1))
        attend(sb_ref, first + 1)
        return carry

    lax.fori_loop(0, n_tiles // 2, pair, 0)

    @pl.when(lax.rem(n_tiles, 2) == 1)
    def _():
        attend(sa_ref, n_tiles - 1)

    o_sel = acc_ref[...]
    inv_s = 1.0 / o_sel[dv:dv + 1, :]
    for h in range(r):
        cols = slice(h * dv, (h + 1) * dv)
        o_ref[:, cols] = (mix_ref[:, cols] + gated(o_sel, inv_s, zs_ref[:, cols].astype(F32), 1, h)).astype(BF16)


def _overlap_matrix_t(n_blk, n_sel):
    a, b = SEL_LEN // CMP_STRIDE, CMP_LEN // CMP_STRIDE
    span = a + b - 2
    n = np.arange(n_blk)[None, :]
    j = np.arange(LANES)[:, None]
    diff = n - a * j
    ov = np.where((diff >= 0) & (diff <= span), np.minimum(np.minimum(diff, span - diff), min(a, b) - 1) + 1, 0)
    ov = np.where((n < n_blk - 1) & (j < n_sel), ov, 0)
    return jnp.asarray(ov, BF16)


def _nsa_attention(qp, zp, xm, w_t, gate_block, kc, vct, kv, vt, bsz, seq):
    g, r, dv, dkp, tq, tk = NSA_GROUPS, NSA_HPG, NSA_DV, NSA_DKP, ATT_TQ, SEL_TK
    d = xm.shape[1]
    nq = seq // tq
    n_blk = seq // CMP_STRIDE
    n_sel = seq // SEL_LEN
    ovt = _overlap_matrix_t(n_blk, n_sel)
    row = lambda b, gg, i: b * nq + i
    z_spec = lambda br: pl.BlockSpec((tq, r * dv), lambda b, gg, i: (row(b, gg, i), br * g + gg))
    ks0, kw0 = 0, g * (dkp + dv) // dkp
    return pl.pallas_call(
        _nsa_attn_kernel,
        grid=(bsz, g, nq),
        in_specs=[pl.BlockSpec((tq, r * dkp), lambda b, gg, i: (row(b, gg, i), gg)),
                  z_spec(0), z_spec(1), z_spec(2),
                  pl.BlockSpec((tq, d), lambda b, gg, i: (row(b, gg, i), 0)),
                  pl.BlockSpec((LANES, d), lambda b, gg, i: (gate_block, 0)),
                  pl.BlockSpec((1, 1, n_blk, dkp), lambda b, gg, i: (b, gg, 0, 0)),
                  pl.BlockSpec((1, 1, V_ROWS, n_blk), lambda b, gg, i: (b, gg, 0, 0)),
                  pl.BlockSpec(ovt.shape, lambda b, gg, i: (0, 0)),
                  pl.BlockSpec((seq, dkp), lambda b, gg, i: (b, ks0 + gg)),
                  pl.BlockSpec((1, 1, V_ROWS, seq), lambda b, gg, i: (b, gg, 0, 0)),
                  pl.BlockSpec((seq, dkp), lambda b, gg, i: (b, kw0 + gg)),
                  pl.BlockSpec((1, 1, V_ROWS, seq), lambda b, gg, i: (b, g + gg, 0, 0))],
        out_specs=pl.BlockSpec((tq, r * dv), lambda b, gg, i: (row(b, gg, i), gg)),
        out_shape=jax.ShapeDtypeStruct((bsz * seq, g * r * dv), BF16),
        scratch_shapes=[pltpu.VMEM((n_sel, tq), F32),
                        pltpu.VMEM((tk, r * tq), F32),
                        pltpu.VMEM((tk, r * tq), F32),
                        pltpu.VMEM((1, r * tq), F32),
                        pltpu.VMEM((V_ROWS, r * tq), F32),
                        pltpu.VMEM((tq, r * dv), F32),
                        pltpu.VMEM((LANES, d), BF16),
                        pltpu.VMEM((LANES, tq), F32)],
        compiler_params=_params(3),
        name="nsa_attn",
    )(qp, zp, zp, zp, xm, w_t, kc, vct, ovt, kv, vt, kv, vt)


def _nsa_layer(x, xm, shared, w_in, w_out, gain, bias, bsz, seq, *, last):
    kc, vct, kv, vt = shared
    h, g, r, dk, dv, dkp = NSA_HEADS, NSA_GROUPS, NSA_HPG, NSA_DK, NSA_DV, NSA_DKP
    nq = h * dk
    nz = N_BRANCH * h * dv
    tn, tm = 1024, _proj_rows(xm)
    w_t = w_in.T
    qp = _proj(xm, w_t, n_tiles=g, col_block0=0, tm=tm, tn=r * dk, out_dtype=BF16, w_transposed=True,
               head_pad=(r, dk, dkp), out_scale=(dk ** -0.5) * math.log2(math.e))
    zp = _proj(xm, w_t, n_tiles=nz // tn, col_block0=nq // tn, tm=tm, tn=tn, out_dtype=BF16, w_transposed=True,
               silu_out=True)
    mixed = _nsa_attention(qp, zp, xm, w_t, (nq + nz) // LANES, kc, vct, kv, vt, bsz, seq)
    return _out_ln(mixed, w_out.astype(BF16), x, gain, bias, with_bf16=not last)


def kernel(x, positions, ret_w_in_0, ret_w_out_0, ln_g_0, ln_b_0, ret_w_in_1, ret_w_out_1, ln_g_1, ln_b_1,
           nsa_w_kv, nsa_pe_k, nsa_pe_v, nsa_w_ck1, nsa_w_ck2, nsa_w_cv1, nsa_w_cv2,
           nsa_w_in_2, nsa_w_out_2, ln_g_2, ln_b_2, nsa_w_in_3, nsa_w_out_3, ln_g_3, ln_b_3):
    bsz, seq, d = x.shape
    h = x.reshape(bsz * seq, d)
    cos, sin = _rope_tables(positions)
    h, hm = _retention_layer(h, h, cos, sin, ret_w_in_0, ret_w_out_0, ln_g_0, ln_b_0, bsz, seq)
    h, hm = _retention_layer(h, hm, cos, sin, ret_w_in_1, ret_w_out_1, ln_g_1, ln_b_1, bsz, seq)
    shared = _nsa_shared_kv(hm, nsa_w_kv, nsa_pe_k, nsa_pe_v, nsa_w_ck1, nsa_w_ck2, nsa_w_cv1, nsa_w_cv2, bsz, seq)
    h, hm = _nsa_layer(h, hm, shared, nsa_w_in_2, nsa_w_out_2, ln_g_2, ln_b_2, bsz, seq, last=False)
    h = _nsa_layer(h, hm, shared, nsa_w_in_3, nsa_w_out_3, ln_g_3, ln_b_3, bsz, seq, last=True)
    return h.reshape(bsz, seq, d)
```

```python
import functools
import math

import numpy as np
import jax
import jax.numpy as jnp
from jax import lax
from jax.experimental import pallas as pl
from jax.experimental.pallas import tpu as pltpu

D_MODEL = 2048
DEPTH = 4
N_A_LAYERS = DEPTH // 2
RET_HEADS = 8
RET_DK = D_MODEL // RET_HEADS
RET_DV = 2 * D_MODEL // RET_HEADS
RET_CHUNK = 128
RET_THETA_BASE = 10000.0
NSA_HEADS = 16
NSA_GROUPS = 4
NSA_HPG = NSA_HEADS // NSA_GROUPS
NSA_DV = D_MODEL // NSA_HEADS
NSA_DK = 3 * NSA_DV // 2
NSA_DKP = 256
CMP_LEN = 32
CMP_STRIDE = 16
CMP_HID = 2 * NSA_DV
SEL_LEN = 64
SEL_TOPK = 16
WINDOW = 512
N_BRANCH = 3
ALPHA = (2.0 * DEPTH) ** 0.25
NEG_INF = -1e30
FORCE_SCORE = 1e9
LN_EPS = 1e-5

LANES = 128
SUBLANES = 8
BF16_SUBLANES = 16
ATT_TQ = 256
SEL_TK = 512
ATT_GPS = 2
V_ROWS = NSA_DV + BF16_SUBLANES
VMEM_LIMIT = 56 * 1024 * 1024

F32 = jnp.float32
BF16 = jnp.bfloat16
NT_DIMS = (((1,), (1,)), ((), ()))
TN_DIMS = (((0,), (0,)), ((), ()))


def _params(n_axes):
    return pltpu.CompilerParams(dimension_semantics=("arbitrary",) * n_axes, vmem_limit_bytes=VMEM_LIMIT)


def _sigmoid(x):
    return 0.5 * jnp.tanh(0.5 * x) + 0.5


def _silu(x):
    half = 0.5 * x
    return half + half * jnp.tanh(half)


def _proj_kernel(x_ref, w_ref, *rest, out_scale, w_transposed, head_pad):
    perm_ref = rest[0] if len(rest) == 3 else None
    o_ref, wbf_ref = rest[-2:]

    @pl.when(pl.program_id(1) == 0)
    def _():
        if perm_ref is None:
            wbf_ref[...] = w_ref[...].astype(BF16)
        else:
            blk = perm_ref.shape[0]
            for c in range(w_ref.shape[1] // blk):
                cols = slice(c * blk, (c + 1) * blk)
                wbf_ref[:, cols] = jnp.dot(w_ref[:, cols].astype(BF16), perm_ref[...],
                                           preferred_element_type=F32).astype(BF16)

    xb = x_ref[...]
    if xb.dtype != BF16:
        xb = xb.astype(BF16)
    if w_transposed:
        acc = lax.dot_general(xb, wbf_ref[...], NT_DIMS, preferred_element_type=F32)
    else:
        acc = jnp.dot(xb, wbf_ref[...], preferred_element_type=F32)
    if out_scale is not None:
        acc = acc * out_scale
    if head_pad is not None:
        n_heads, width, padded = head_pad
        pieces = []
        for h in range(n_heads):
            pieces += [acc[:, h * width:(h + 1) * width], jnp.zeros((acc.shape[0], padded - width), F32)]
        if acc.shape[1] > n_heads * width:
            pieces.append(acc[:, n_heads * width:])
        acc = jnp.concatenate(pieces, axis=1)
    o_ref[...] = acc.astype(o_ref.dtype)


def _proj(x, w, *, n_tiles, col_block0, tm, tn, out_dtype, perm=None, out_scale=None, w_transposed=False,
          head_pad=None):
    m, k = x.shape
    tn_out = tn if head_pad is None else tn + head_pad[0] * (head_pad[2] - head_pad[1])
    if w_transposed:
        w_spec = pl.BlockSpec((tn, k), lambda j, i: (j + col_block0, 0))
    else:
        w_spec = pl.BlockSpec((k, tn), lambda j, i: (0, j + col_block0))
    in_specs = [pl.BlockSpec((tm, k), lambda j, i: (i, 0)), w_spec]
    args = [x, w]
    if perm is not None:
        in_specs.append(pl.BlockSpec(perm.shape, lambda j, i: (0, 0)))
        args.append(perm)
    return pl.pallas_call(
        functools.partial(_proj_kernel, out_scale=out_scale, w_transposed=w_transposed, head_pad=head_pad),
        grid=(n_tiles, m // tm),
        in_specs=in_specs,
        out_specs=pl.BlockSpec((tm, tn_out), lambda j, i: (i, j)),
        out_shape=jax.ShapeDtypeStruct((m, n_tiles * tn_out), out_dtype),
        scratch_shapes=[pltpu.VMEM((tn, k) if w_transposed else (k, tn), BF16)],
        compiler_params=_params(2),
        name="proj",
    )(*args)


OUT_LN_SPLIT = 4


def _out_ln_kernel(a_ref, w_ref, x_ref, g_ref, b_ref, o_ref, *o16_ref):
    rows = a_ref.shape[0] // OUT_LN_SPLIT
    for part in range(OUT_LN_SPLIT):
        sl = slice(part * rows, (part + 1) * rows)
        u = ALPHA * x_ref[sl, :] + jnp.dot(a_ref[sl, :], w_ref[...], preferred_element_type=F32)
        mu = jnp.mean(u, axis=-1, keepdims=True)
        var = jnp.mean(jnp.square(u - mu), axis=-1, keepdims=True)
        y = (u - mu) * lax.rsqrt(var + LN_EPS) * g_ref[...] + b_ref[...]
        o_ref[sl, :] = y
        if o16_ref:
            o16_ref[0][sl, :] = y.astype(BF16)


def _out_ln(a, w, x, gain, bias, *, tm=512, with_bf16=True):
    m, ka = a.shape
    d = w.shape[1]
    o_spec = pl.BlockSpec((tm, d), lambda i: (i, 0))
    return pl.pallas_call(
        _out_ln_kernel,
        grid=(m // tm,),
        in_specs=[pl.BlockSpec((tm, ka), lambda i: (i, 0)),
                  pl.BlockSpec((ka, d), lambda i: (0, 0), pipeline_mode=pl.Buffered(1)),
                  pl.BlockSpec((tm, d), lambda i: (i, 0)),
                  pl.BlockSpec((1, d), lambda i: (0, 0)),
                  pl.BlockSpec((1, d), lambda i: (0, 0))],
        out_specs=[o_spec, o_spec] if with_bf16 else o_spec,
        out_shape=([jax.ShapeDtypeStruct((m, d), F32), jax.ShapeDtypeStruct((m, d), BF16)] if with_bf16
                   else jax.ShapeDtypeStruct((m, d), F32)),
        compiler_params=_params(1),
        name="out_ln",
    )(a, w, x, gain.reshape(1, d), bias.reshape(1, d))


def _rope_kernel(pos_ref, freq_ref, cos_ref, sin_ref):
    ang = pos_ref[...].astype(F32) * freq_ref[...]
    cos_ref[...] = jnp.cos(ang)
    sin_ref[...] = jnp.sin(ang)


def _rope_tables(positions, tm=1024):
    t = positions.size
    half = RET_DK // 2
    inv_freq = 1.0 / (RET_THETA_BASE ** jnp.linspace(0.0, 1.0, half, dtype=F32))
    return pl.pallas_call(
        _rope_kernel,
        grid=(t // tm,),
        in_specs=[pl.BlockSpec((tm, 1), lambda i: (i, 0)), pl.BlockSpec((1, half), lambda i: (0, 0))],
        out_specs=[pl.BlockSpec((tm, half), lambda i: (i, 0))] * 2,
        out_shape=[jax.ShapeDtypeStruct((t, half), F32)] * 2,
        compiler_params=_params(1),
        name="rope_tables",
    )(positions.reshape(t, 1), inv_freq.reshape(1, half))


def _retention_kernel(q_ref, k_ref, v_ref, z_ref, cos_ref, sin_ref, dintra_ref, dq_ref, dk_ref, dch_ref,
                      o_ref, state_ref, *, n_sub, n_heads):
    c = RET_CHUNK
    half = RET_DK // 2

    @pl.when(pl.program_id(2) == 0)
    def _():
        state_ref[...] = jnp.zeros_like(state_ref)

    def rotate(t, cosv, sinv):
        even, odd = t[:, :half], t[:, half:]
        return jnp.concatenate([even * cosv - odd * sinv, even * sinv + odd * cosv], axis=1)

    for j in range(n_sub):
        rows = pl.ds(j * c, c)
        cosv, sinv = cos_ref[rows, :], sin_ref[rows, :]
        for hh in range(n_heads):
            kcols = slice(hh * RET_DK, (hh + 1) * RET_DK)
            vcols = slice(hh * RET_DV, (hh + 1) * RET_DV)
            qr = rotate(q_ref[rows, kcols].astype(F32), cosv, sinv)
            kr = rotate(k_ref[rows, kcols].astype(F32), cosv, sinv)
            qb, kb = qr.astype(BF16), kr.astype(BF16)
            kdb = (kr * dk_ref[hh]).astype(BF16)
            v = v_ref[rows, vcols]
            state = state_ref[hh]
            scores = lax.dot_general(qb, kb, NT_DIMS, preferred_element_type=F32) * dintra_ref[hh]
            out = (jnp.dot(scores.astype(BF16), v, preferred_element_type=F32)
                   + jnp.dot(qb, state.astype(BF16), preferred_element_type=F32) * dq_ref[hh])
            state_ref[hh] = state * dch_ref[hh] + lax.dot_general(kdb, v, TN_DIMS, preferred_element_type=F32)
            mu = jnp.mean(out, axis=-1, keepdims=True)
            var = jnp.mean(jnp.square(out - mu), axis=-1, keepdims=True)
            normed = (out - mu) * lax.rsqrt(var + LN_EPS)
            o_ref[rows, vcols] = (normed * _silu(z_ref[rows, vcols].astype(F32))).astype(BF16)


def _retention_decays():
    h, c = RET_HEADS, RET_CHUNK
    k_scale = RET_DK ** -0.5
    assert math.frexp(k_scale)[0] == 0.5
    log_gamma = jnp.log1p(-jnp.exp2(-5.0 - jnp.arange(h, dtype=F32)))
    idx = jnp.arange(c, dtype=F32)
    rel = idx[:, None] - idx[None, :]
    dintra = jnp.where(rel >= 0, jnp.exp(log_gamma[:, None, None] * jnp.maximum(rel, 0.0)), 0.0) * k_scale
    dq = jnp.exp(log_gamma[:, None] * (idx + 1.0))
    dk = jnp.exp(log_gamma[:, None] * (c - 1.0 - idx)) * k_scale
    dch = jnp.exp(log_gamma * c)
    return (dintra,
            jnp.broadcast_to(dq[:, :, None], (h, c, RET_DV)),
            jnp.broadcast_to(dk[:, :, None], (h, c, RET_DK)),
            jnp.broadcast_to(dch[:, None, None], (h, 1, RET_DV)))


def _retention(qk, vz, cos, sin, bsz, seq, *, ts=1024, hps=4):
    h, dk, dv, c = RET_HEADS, RET_DK, RET_DV, RET_CHUNK
    ns = seq // ts
    ng = h // hps
    dintra, dq, dkt, dch = _retention_decays()
    row = lambda b, hh, s: b * ns + s
    return pl.pallas_call(
        functools.partial(_retention_kernel, n_sub=ts // c, n_heads=hps),
        grid=(bsz, ng, ns),
        in_specs=[pl.BlockSpec((ts, hps * dk), lambda b, hh, s: (row(b, hh, s), hh)),
                  pl.BlockSpec((ts, hps * dk), lambda b, hh, s: (row(b, hh, s), ng + hh)),
                  pl.BlockSpec((ts, hps * dv), lambda b, hh, s: (row(b, hh, s), hh)),
                  pl.BlockSpec((ts, hps * dv), lambda b, hh, s: (row(b, hh, s), ng + hh)),
                  pl.BlockSpec((ts, dk // 2), lambda b, hh, s: (row(b, hh, s), 0)),
                  pl.BlockSpec((ts, dk // 2), lambda b, hh, s: (row(b, hh, s), 0)),
                  pl.BlockSpec((hps, c, c), lambda b, hh, s: (hh, 0, 0)),
                  pl.BlockSpec((hps, c, dv), lambda b, hh, s: (hh, 0, 0)),
                  pl.BlockSpec((hps, c, dk), lambda b, hh, s: (hh, 0, 0)),
                  pl.BlockSpec((hps, 1, dv), lambda b, hh, s: (hh, 0, 0))],
        out_specs=pl.BlockSpec((ts, hps * dv), lambda b, hh, s: (row(b, hh, s), hh)),
        out_shape=jax.ShapeDtypeStruct((bsz * seq, h * dv), BF16),
        scratch_shapes=[pltpu.VMEM((hps, dk, dv), F32)],
        compiler_params=_params(3),
        name="retention",
    )(qk, qk, vz, vz, cos, sin, dintra, dq, dkt, dch)


def _deinterleave_matrix(width):
    p = np.zeros((width, width), np.float32)
    j = np.arange(width // 2)
    p[2 * j, j] = 1.0
    p[2 * j + 1, width // 2 + j] = 1.0
    return jnp.asarray(p, BF16)


def _proj_rows(xm):
    return 2048 if xm.dtype == BF16 else 1024


def _retention_layer(x, xm, cos, sin, w_in, w_out, gain, bias, bsz, seq):
    h, dk, dv = RET_HEADS, RET_DK, RET_DV
    tn, tm = 1024, _proj_rows(xm)
    qk = _proj(xm, w_in, n_tiles=2 * h * dk // tn, col_block0=0, tm=tm, tn=tn, out_dtype=BF16,
               perm=_deinterleave_matrix(dk))
    vz = _proj(xm, w_in, n_tiles=2 * h * dv // tn, col_block0=2 * h * dk // tn, tm=tm, tn=tn, out_dtype=BF16)
    mixed = _retention(qk, vz, cos, sin, bsz, seq)
    return _out_ln(mixed, w_out.astype(BF16), x, gain, bias)


def _compress_kernel(*refs, transposed):
    x_refs, (pe_ref, w1_ref, w2_ref, o_ref) = refs[:-4], refs[-4:]
    n_blk = o_ref.shape[3] if transposed else o_ref.shape[2]
    half = CMP_LEN // 2
    lo, hi = [], []
    for l in range(half):
        rows = jnp.concatenate([x_ref[pl.ds(l, n_blk, stride=CMP_STRIDE), :] for x_ref in x_refs], axis=1)
        lo.append((rows + pe_ref[l:l + 1, :]).astype(BF16))
        hi.append((rows + pe_ref[half + l:half + l + 1, :]).astype(BF16))
    acc_lo = jnp.dot(jnp.concatenate(lo, axis=1), w1_ref[0], preferred_element_type=F32)
    acc_hi = jnp.dot(jnp.concatenate(hi, axis=1), w1_ref[1], preferred_element_type=F32)
    hid = acc_lo + pltpu.roll(acc_hi, shift=n_blk - 1, axis=0)
    out = jnp.dot(_silu(hid).astype(BF16), w2_ref[...], preferred_element_type=F32)
    keep = lax.broadcasted_iota(jnp.int32, out.shape, 0) < n_blk - 1
    out = jnp.where(keep, out, 0.0)
    if transposed:
        width = out.shape[1]
        o_ref[0, 0, :width, :] = out.T.astype(BF16)
        o_ref[0, 0, width:, :] = jnp.ones((o_ref.shape[2] - width, n_blk), BF16)
    else:
        o_ref[0, 0] = out.astype(BF16)


def _compress(kvc, pe, w1, w2, bsz, seq, *, width, col_block0, out_rows=None):
    g = NSA_GROUPS
    n_blk = seq // CMP_STRIDE
    n_slabs = width // LANES
    out_block = (1, 1, n_blk, w2.shape[1]) if out_rows is None else (1, 1, out_rows, n_blk)
    slab = lambda j: pl.BlockSpec((seq, LANES), lambda b, gg: (b, (col_block0 + gg) * n_slabs + j))
    return pl.pallas_call(
        functools.partial(_compress_kernel, transposed=out_rows is not None),
        grid=(bsz, g),
        in_specs=[slab(j) for j in range(n_slabs)] + [
                  pl.BlockSpec(pe.shape, lambda b, gg: (0, 0)),
                  pl.BlockSpec(w1.shape, lambda b, gg: (0, 0, 0)),
                  pl.BlockSpec(w2.shape, lambda b, gg: (0, 0))],
        out_specs=pl.BlockSpec(out_block, lambda b, gg: (b, gg, 0, 0)),
        out_shape=jax.ShapeDtypeStruct((bsz, g) + out_block[2:], BF16),
        compiler_params=_params(2),
        name="compress",
    )(*([kvc] * n_slabs), pe, w1, w2)


def _values_t_kernel(v_ref, o_ref):
    o_ref[0, 0, :NSA_DV, :] = v_ref[...].astype(F32).T.astype(BF16)
    o_ref[0, 0, NSA_DV:, :] = jnp.ones((V_ROWS - NSA_DV, v_ref.shape[0]), BF16)


def _values_t(kv, bsz, seq):
    g, dv, dkp = NSA_GROUPS, NSA_DV, NSA_DKP
    branch_blocks = g * (dkp + dv) // dv
    v0 = g * dkp // dv
    return pl.pallas_call(
        _values_t_kernel,
        grid=(bsz, 2 * g),
        in_specs=[pl.BlockSpec((seq, dv), lambda b, j: (b, v0 + (j // g) * branch_blocks + j % g))],
        out_specs=pl.BlockSpec((1, 1, V_ROWS, seq), lambda b, j: (b, j, 0, 0)),
        out_shape=jax.ShapeDtypeStruct((bsz, 2 * g, V_ROWS, seq), BF16),
        compiler_params=_params(2),
        name="values_t",
    )(kv)


def _pad_last(a, width):
    return jnp.pad(a, [(0, 0)] * (a.ndim - 1) + [(0, width - a.shape[-1])])


def _nsa_shared_kv(x, w_kv, pe_k, pe_v, w_ck1, w_ck2, w_cv1, w_cv2, bsz, seq):
    g, dk, dv, dkp = NSA_GROUPS, NSA_DK, NSA_DV, NSA_DKP
    tn = g * (dk + dv)
    pad = (g, dk, dkp)
    kvc = _proj(x, w_kv, n_tiles=1, col_block0=0, tm=1024, tn=tn, out_dtype=F32, head_pad=pad)
    kv = _proj(x, w_kv, n_tiles=2, col_block0=1, tm=1024, tn=tn, out_dtype=BF16, head_pad=pad)
    w1k = _pad_last(w_ck1.reshape(CMP_LEN, dk, CMP_HID).transpose(0, 2, 1), dkp).transpose(0, 2, 1).astype(BF16)
    w1k = w1k.reshape(2, CMP_LEN // 2 * dkp, CMP_HID)
    w1v = w_cv1.reshape(2, CMP_LEN // 2 * dv, CMP_HID).astype(BF16)
    kc = _compress(kvc, _pad_last(pe_k, dkp), w1k, _pad_last(w_ck2, dkp).astype(BF16), bsz, seq,
                   width=dkp, col_block0=0)
    vct = _compress(kvc, pe_v, w1v, w_cv2.astype(BF16), bsz, seq, width=dv, col_block0=g * dkp // dv,
                    out_rows=V_ROWS)
    return kc, vct, kv, _values_t(kv, bsz, seq)


def _nsa_attn_kernel(q_ref, zc_ref, zs_ref, zw_ref, x_ref, wg_ref, kc_ref, vct_ref, ovt_ref,
                     ks_ref, vst_ref, kw_ref, vwt_ref, o_ref,
                     bias_ref, sa_ref, sb_ref, m_ref, acc_ref, mix_ref, zsil_ref, wgb_ref, gates_ref, *, gps):
    tq, r, dv, dkp, tk = q_ref.shape[0], NSA_HPG, NSA_DV, NSA_DKP, SEL_TK
    lanes = r * tq
    n_blk = kc_ref.shape[2]
    n_sel = n_blk * CMP_STRIDE // SEL_LEN
    q0 = pl.program_id(2) * tq

    def qpos(shape):
        return q0 + lax.rem(lax.broadcasted_iota(jnp.int32, shape, 1), tq)

    def all_heads(tile):
        return jnp.concatenate([tile] * r, axis=1)

    def weights(s):
        return jnp.exp2(s - jnp.max(s, axis=0, keepdims=True)).astype(BF16)

    @pl.when((pl.program_id(0) == 0) & (pl.program_id(1) == 0) & (pl.program_id(2) == 0))
    def _():
        real = lax.broadcasted_iota(jnp.int32, wg_ref.shape, 0) < N_BRANCH * NSA_HEADS
        wgb_ref[...] = jnp.where(real, wg_ref[...], 0.0).astype(BF16)

    gates_ref[...] = _sigmoid(lax.dot_general(wgb_ref[...], x_ref[...], NT_DIMS, preferred_element_type=F32))

    def gated(gl, o_aug, inv_l, silu_z, br, h):
        cols = slice(h * tq, (h + 1) * tq)
        head = (pl.program_id(1) * gps + gl) * r + h
        row_scale = gates_ref[pl.ds(br * NSA_HEADS + head, 1), :] * inv_l[:, cols]
        return (o_aug[:dv, cols] * row_scale).T * silu_z

    q4s = []
    for gl in range(gps):
        q4 = jnp.concatenate([q_ref[:, (gl * r + h) * dkp:(gl * r + h + 1) * dkp] for h in range(r)], axis=0)
        q4s.append(q4)

        s = lax.dot_general(kc_ref[0, gl], q4, NT_DIMS, preferred_element_type=F32)
        blk_end = lax.broadcasted_iota(jnp.int32, (n_blk, tq), 0) * CMP_STRIDE + (CMP_LEN - 1)
        s = s + all_heads(jnp.where(blk_end <= qpos((n_blk, tq)), 0.0, NEG_INF))
        eb = weights(s)
        o_cmp = jnp.dot(vct_ref[0, gl], eb, preferred_element_type=F32)
        inv_c = jnp.where(qpos((1, lanes)) >= CMP_LEN - 1, 1.0 / o_cmp[dv:dv + 1, :], 0.0)

        imp = sum(jnp.dot(ovt_ref[...], eb[:, h * tq:(h + 1) * tq], preferred_element_type=F32)
                  * inv_c[:, h * tq:(h + 1) * tq] for h in range(r))
        imp = imp[:n_sel, :]
        blk = lax.broadcasted_iota(jnp.int32, imp.shape, 0)
        cur = (q0 + lax.broadcasted_iota(jnp.int32, imp.shape, 1)) // SEL_LEN
        forced = (blk == 0) | (blk == cur) | (blk == cur - 1)
        score = jnp.where(forced, FORCE_SCORE, jnp.where(blk <= cur, imp, -1.0))
        groups = [score[g0:g0 + SUBLANES, :] for g0 in range(0, n_sel, SUBLANES)]
        ranks = [jnp.zeros((SUBLANES, tq), F32) for _ in groups]
        for i in range(n_sel):
            row = score[i:i + 1, :]
            for gi, grp in enumerate(groups):
                g0 = gi * SUBLANES
                if g0 > i:
                    ahead = row >= grp
                elif g0 + SUBLANES - 1 < i:
                    ahead = row > grp
                else:
                    after = lax.broadcasted_iota(jnp.int32, grp.shape, 0) + g0 > i
                    ahead = (row > grp) | (after & (row == grp))
                ranks[gi] = ranks[gi] + jnp.where(ahead, 1.0, 0.0)
        rank = jnp.concatenate(ranks, axis=0)
        bias_ref[gl] = jnp.where(rank < float(min(SEL_TOPK, n_sel)), 0.0, NEG_INF)

        span = WINDOW + tq
        k0 = pl.multiple_of(jnp.maximum(q0 - WINDOW, 0), tq)
        s = lax.dot_general(kw_ref[pl.ds(k0, span), gl * dkp:(gl + 1) * dkp], q4, NT_DIMS,
                            preferred_element_type=F32)
        dist = qpos((span, tq)) - (k0 + lax.broadcasted_iota(jnp.int32, (span, tq), 0))
        s = s + all_heads(jnp.where((dist >= 0) & (dist < WINDOW), 0.0, NEG_INF))
        o_win = jnp.dot(vwt_ref[0, gl, :, pl.ds(k0, span)], weights(s), preferred_element_type=F32)
        inv_w = 1.0 / o_win[dv:dv + 1, :]

        for h in range(r):
            cols = slice((gl * r + h) * dv, (gl * r + h + 1) * dv)
            mix_ref[:, cols] = (gated(gl, o_cmp, inv_c, _silu(zc_ref[:, cols].astype(F32)), 0, h)
                                + gated(gl, o_win, inv_w, _silu(zw_ref[:, cols].astype(F32)), 2, h))
    zsil_ref[...] = _silu(zs_ref[...].astype(F32))

    m_ref[...] = jnp.full(m_ref.shape, NEG_INF, F32)
    acc_ref[...] = jnp.zeros_like(acc_ref)
    blocks_per_tile = tk // SEL_LEN
    n_tiles = q0 // tk + 1
    causal_gap = (q0 + lax.broadcasted_iota(jnp.int32, (tk, tq), 1)) - lax.broadcasted_iota(jnp.int32, (tk, tq), 0)

    def scores(gl, tile):
        kbase = pl.multiple_of(tile * tk, tk)
        rows = bias_ref[gl, pl.ds(pl.multiple_of(tile * blocks_per_tile, blocks_per_tile), blocks_per_tile), :]
        bias = jnp.concatenate([jnp.broadcast_to(rows[j:j + 1, :], (SEL_LEN, tq)) for j in range(blocks_per_tile)],
                               axis=0)
        bias = jnp.where(causal_gap >= kbase, bias, NEG_INF)
        return (lax.dot_general(ks_ref[pl.ds(kbase, tk), gl * dkp:(gl + 1) * dkp], q4s[gl], NT_DIMS,
                                preferred_element_type=F32) + all_heads(bias))

    def attend(gl, s_ref, tile):
        kbase = pl.multiple_of(tile * tk, tk)
        s = s_ref[gl]
        m_prev = m_ref[gl]
        m_new = jnp.maximum(m_prev, jnp.max(s, axis=0, keepdims=True))
        pv = jnp.dot(vst_ref[0, gl, :, pl.ds(kbase, tk)], jnp.exp2(s - m_new).astype(BF16),
                     preferred_element_type=F32)
        acc_ref[gl] = jnp.exp2(m_prev - m_new) * acc_ref[gl] + pv
        m_ref[gl] = m_new

    for gl in range(gps):
        sa_ref[gl] = scores(gl, 0)

    def pair(jp, carry):
        first = 2 * jp
        for gl in range(gps):
            sb_ref[gl] = scores(gl, first + 1)
            attend(gl, sa_ref, first)
        for gl in range(gps):
            sa_ref[gl] = scores(gl, jnp.minimum(first + 2, n_tiles - 1))
            attend(gl, sb_ref, first + 1)
        return carry

    lax.fori_loop(0, n_tiles // 2, pair, 0)

    @pl.when(lax.rem(n_tiles, 2) == 1)
    def _():
        for gl in range(gps):
            attend(gl, sa_ref, n_tiles - 1)

    for gl in range(gps):
        o_sel = acc_ref[gl]
        inv_s = 1.0 / o_sel[dv:dv + 1, :]
        for h in range(r):
            cols = slice((gl * r + h) * dv, (gl * r + h + 1) * dv)
            o_ref[:, cols] = (mix_ref[:, cols] + gated(gl, o_sel, inv_s, zsil_ref[:, cols], 1, h)).astype(BF16)


def _overlap_matrix_t(n_blk, n_sel):
    a, b = SEL_LEN // CMP_STRIDE, CMP_LEN // CMP_STRIDE
    span = a + b - 2
    n = np.arange(n_blk)[None, :]
    j = np.arange(LANES)[:, None]
    diff = n - a * j
    ov = np.where((diff >= 0) & (diff <= span), np.minimum(np.minimum(diff, span - diff), min(a, b) - 1) + 1, 0)
    ov = np.where((n < n_blk - 1) & (j < n_sel), ov, 0)
    return jnp.asarray(ov, BF16)


def _nsa_attention(qp, zp, xm, w_t, gate_block, kc, vct, kv, vt, bsz, seq, *, gps=ATT_GPS):
    g, r, dv, dkp, tq, tk = NSA_GROUPS, NSA_HPG, NSA_DV, NSA_DKP, ATT_TQ, SEL_TK
    d = xm.shape[1]
    ng = g // gps
    nq = seq // tq
    n_blk = seq // CMP_STRIDE
    n_sel = seq // SEL_LEN
    ovt = _overlap_matrix_t(n_blk, n_sel)
    row = lambda b, gg, i: b * nq + i
    z_spec = lambda br: pl.BlockSpec((tq, gps * r * dv), lambda b, gg, i: (row(b, gg, i), br * ng + gg))
    kw0 = g * (dkp + dv) // (gps * dkp)
    return pl.pallas_call(
        functools.partial(_nsa_attn_kernel, gps=gps),
        grid=(bsz, ng, nq),
        in_specs=[pl.BlockSpec((tq, gps * r * dkp), lambda b, gg, i: (row(b, gg, i), gg)),
                  z_spec(0), z_spec(1), z_spec(2),
                  pl.BlockSpec((tq, d), lambda b, gg, i: (row(b, gg, i), 0)),
                  pl.BlockSpec((LANES, d), lambda b, gg, i: (gate_block, 0)),
                  pl.BlockSpec((1, gps, n_blk, dkp), lambda b, gg, i: (b, gg, 0, 0)),
                  pl.BlockSpec((1, gps, V_ROWS, n_blk), lambda b, gg, i: (b, gg, 0, 0)),
                  pl.BlockSpec(ovt.shape, lambda b, gg, i: (0, 0)),
                  pl.BlockSpec((seq, gps * dkp), lambda b, gg, i: (b, gg)),
                  pl.BlockSpec((1, gps, V_ROWS, seq), lambda b, gg, i: (b, gg, 0, 0)),
                  pl.BlockSpec((seq, gps * dkp), lambda b, gg, i: (b, kw0 + gg)),
                  pl.BlockSpec((1, gps, V_ROWS, seq), lambda b, gg, i: (b, ng + gg, 0, 0))],
        out_specs=pl.BlockSpec((tq, gps * r * dv), lambda b, gg, i: (row(b, gg, i), gg)),
        out_shape=jax.ShapeDtypeStruct((bsz * seq, g * r * dv), BF16),
        scratch_shapes=[pltpu.VMEM((gps, n_sel, tq), F32),
                        pltpu.VMEM((gps, tk, r * tq), F32),
                        pltpu.VMEM((gps, tk, r * tq), F32),
                        pltpu.VMEM((gps, 1, r * tq), F32),
                        pltpu.VMEM((gps, V_ROWS, r * tq), F32),
                        pltpu.VMEM((tq, gps * r * dv), F32),
                        pltpu.VMEM((tq, gps * r * dv), F32),
                        pltpu.VMEM((LANES, d), BF16),
                        pltpu.VMEM((LANES, tq), F32)],
        compiler_params=_params(3),
        name="nsa_attn",
    )(qp, zp, zp, zp, xm, w_t, kc, vct, ovt, kv, vt, kv, vt)


def _nsa_layer(x, xm, shared, w_in, w_out, gain, bias, bsz, seq, *, last):
    kc, vct, kv, vt = shared
    h, g, r, dk, dv, dkp = NSA_HEADS, NSA_GROUPS, NSA_HPG, NSA_DK, NSA_DV, NSA_DKP
    nq = h * dk
    nz = N_BRANCH * h * dv
    tn, tm = 1024, _proj_rows(xm)
    w_t = w_in.T
    qp = _proj(xm, w_t, n_tiles=g, col_block0=0, tm=tm, tn=r * dk, out_dtype=BF16, w_transposed=True,
               head_pad=(r, dk, dkp), out_scale=(dk ** -0.5) * math.log2(math.e))
    zp = _proj(xm, w_t, n_tiles=nz // tn, col_block0=nq // tn, tm=tm, tn=tn, out_dtype=BF16, w_transposed=True)
    mixed = _nsa_attention(qp, zp, xm, w_t, (nq + nz) // LANES, kc, vct, kv, vt, bsz, seq)
    return _out_ln(mixed, w_out.astype(BF16), x, gain, bias, with_bf16=not last)


def kernel(x, positions, ret_w_in_0, ret_w_out_0, ln_g_0, ln_b_0, ret_w_in_1, ret_w_out_1, ln_g_1, ln_b_1,
           nsa_w_kv, nsa_pe_k, nsa_pe_v, nsa_w_ck1, nsa_w_ck2, nsa_w_cv1, nsa_w_cv2,
           nsa_w_in_2, nsa_w_out_2, ln_g_2, ln_b_2, nsa_w_in_3, nsa_w_out_3, ln_g_3, ln_b_3):
    bsz, seq, d = x.shape
    h = x.reshape(bsz * seq, d)
    cos, sin = _rope_tables(positions)
    h, hm = _retention_layer(h, h, cos, sin, ret_w_in_0, ret_w_out_0, ln_g_0, ln_b_0, bsz, seq)
    h, hm = _retention_layer(h, hm, cos, sin, ret_w_in_1, ret_w_out_1, ln_g_1, ln_b_1, bsz, seq)
    shared = _nsa_shared_kv(hm, nsa_w_kv, nsa_pe_k, nsa_pe_v, nsa_w_ck1, nsa_w_ck2, nsa_w_cv1, nsa_w_cv2, bsz, seq)
    h, hm = _nsa_layer(h, hm, shared, nsa_w_in_2, nsa_w_out_2, ln_g_2, ln_b_2, bsz, seq, last=False)
    h = _nsa_layer(h, hm, shared, nsa_w_in_3, nsa_w_out_3, ln_g_3, ln_b_3, bsz, seq, last=True)
    return h.reshape(bsz, seq, d)
```

```python
import functools
import math

import numpy as np
import jax
import jax.numpy as jnp
from jax import lax
from jax.experimental import pallas as pl
from jax.experimental.pallas import tpu as pltpu

D_MODEL = 2048
DEPTH = 4
N_A_LAYERS = DEPTH // 2
RET_HEADS = 8
RET_DK = D_MODEL // RET_HEADS
RET_DV = 2 * D_MODEL // RET_HEADS
RET_CHUNK = 128
RET_THETA_BASE = 10000.0
NSA_HEADS = 16
NSA_GROUPS = 4
NSA_HPG = NSA_HEADS // NSA_GROUPS
NSA_DV = D_MODEL // NSA_HEADS
NSA_DK = 3 * NSA_DV // 2
NSA_DKP = 256
CMP_LEN = 32
CMP_STRIDE = 16
CMP_HID = 2 * NSA_DV
SEL_LEN = 64
SEL_TOPK = 16
WINDOW = 512
N_BRANCH = 3
ALPHA = (2.0 * DEPTH) ** 0.25
NEG_INF = -1e30
FORCE_SCORE = 1e9
LN_EPS = 1e-5

LANES = 128
SUBLANES = 8
BF16_SUBLANES = 16
ATT_TQ = 256
SEL_TK = 512
ATT_GPS = 2
V_ROWS = NSA_DV + BF16_SUBLANES
VMEM_LIMIT = 56 * 1024 * 1024

F32 = jnp.float32
BF16 = jnp.bfloat16
NT_DIMS = (((1,), (1,)), ((), ()))
TN_DIMS = (((0,), (0,)), ((), ()))


def _params(n_axes):
    return pltpu.CompilerParams(dimension_semantics=("arbitrary",) * n_axes, vmem_limit_bytes=VMEM_LIMIT)


def _sigmoid(x):
    return 0.5 * jnp.tanh(0.5 * x) + 0.5


def _silu(x):
    half = 0.5 * x
    return half + half * jnp.tanh(half)


def _proj_kernel(x_ref, w_ref, *rest, out_scale, w_transposed, head_pad, perm_tiles):
    perm_ref = rest[0] if len(rest) == 3 else None
    o_ref, wbf_ref = rest[-2:]
    first_row_tile = pl.program_id(1) == 0
    permuted = pl.program_id(0) < perm_tiles if perm_ref is not None else False

    @pl.when(first_row_tile & jnp.logical_not(permuted))
    def _():
        wbf_ref[...] = w_ref[...].astype(BF16)

    if perm_ref is not None:
        @pl.when(first_row_tile & permuted)
        def _():
            blk = perm_ref.shape[0]
            for c in range(w_ref.shape[1] // blk):
                cols = slice(c * blk, (c + 1) * blk)
                wbf_ref[:, cols] = jnp.dot(w_ref[:, cols].astype(BF16), perm_ref[...],
                                           preferred_element_type=F32).astype(BF16)

    xb = x_ref[...]
    if xb.dtype != BF16:
        xb = xb.astype(BF16)
    if w_transposed:
        acc = lax.dot_general(xb, wbf_ref[...], NT_DIMS, preferred_element_type=F32)
    else:
        acc = jnp.dot(xb, wbf_ref[...], preferred_element_type=F32)
    if out_scale is not None:
        acc = acc * out_scale
    if head_pad is not None:
        n_heads, width, padded = head_pad
        pieces = []
        for h in range(n_heads):
            pieces += [acc[:, h * width:(h + 1) * width], jnp.zeros((acc.shape[0], padded - width), F32)]
        if acc.shape[1] > n_heads * width:
            pieces.append(acc[:, n_heads * width:])
        acc = jnp.concatenate(pieces, axis=1)
    o_ref[...] = acc.astype(o_ref.dtype)


def _proj(x, w, *, n_tiles, col_block0, tm, tn, out_dtype, perm=None, out_scale=None, w_transposed=False,
          head_pad=None, perm_tiles=None):
    m, k = x.shape
    tn_out = tn if head_pad is None else tn + head_pad[0] * (head_pad[2] - head_pad[1])
    if w_transposed:
        w_spec = pl.BlockSpec((tn, k), lambda j, i: (j + col_block0, 0))
    else:
        w_spec = pl.BlockSpec((k, tn), lambda j, i: (0, j + col_block0))
    in_specs = [pl.BlockSpec((tm, k), lambda j, i: (i, 0)), w_spec]
    args = [x, w]
    if perm is not None:
        in_specs.append(pl.BlockSpec(perm.shape, lambda j, i: (0, 0)))
        args.append(perm)
    return pl.pallas_call(
        functools.partial(_proj_kernel, out_scale=out_scale, w_transposed=w_transposed, head_pad=head_pad,
                          perm_tiles=n_tiles if perm_tiles is None else perm_tiles),
        grid=(n_tiles, m // tm),
        in_specs=in_specs,
        out_specs=pl.BlockSpec((tm, tn_out), lambda j, i: (i, j)),
        out_shape=jax.ShapeDtypeStruct((m, n_tiles * tn_out), out_dtype),
        scratch_shapes=[pltpu.VMEM((tn, k) if w_transposed else (k, tn), BF16)],
        compiler_params=_params(2),
        name="proj",
    )(*args)


OUT_LN_SPLIT = 4


def _out_ln_kernel(a_ref, w_ref, x_ref, g_ref, b_ref, o_ref, *o16_ref):
    rows = a_ref.shape[0] // OUT_LN_SPLIT
    for part in range(OUT_LN_SPLIT):
        sl = slice(part * rows, (part + 1) * rows)
        u = ALPHA * x_ref[sl, :] + jnp.dot(a_ref[sl, :], w_ref[...], preferred_element_type=F32)
        mu = jnp.mean(u, axis=-1, keepdims=True)
        var = jnp.mean(jnp.square(u - mu), axis=-1, keepdims=True)
        y = (u - mu) * lax.rsqrt(var + LN_EPS) * g_ref[...] + b_ref[...]
        o_ref[sl, :] = y
        if o16_ref:
            o16_ref[0][sl, :] = y.astype(BF16)


def _out_ln(a, w, x, gain, bias, *, tm=512, with_bf16=True):
    m, ka = a.shape
    d = w.shape[1]
    o_spec = pl.BlockSpec((tm, d), lambda i: (i, 0))
    return pl.pallas_call(
        _out_ln_kernel,
        grid=(m // tm,),
        in_specs=[pl.BlockSpec((tm, ka), lambda i: (i, 0)),
                  pl.BlockSpec((ka, d), lambda i: (0, 0), pipeline_mode=pl.Buffered(1)),
                  pl.BlockSpec((tm, d), lambda i: (i, 0)),
                  pl.BlockSpec((1, d), lambda i: (0, 0)),
                  pl.BlockSpec((1, d), lambda i: (0, 0))],
        out_specs=[o_spec, o_spec] if with_bf16 else o_spec,
        out_shape=([jax.ShapeDtypeStruct((m, d), F32), jax.ShapeDtypeStruct((m, d), BF16)] if with_bf16
                   else jax.ShapeDtypeStruct((m, d), F32)),
        compiler_params=_params(1),
        name="out_ln",
    )(a, w, x, gain.reshape(1, d), bias.reshape(1, d))


def _rope_kernel(pos_ref, freq_ref, cos_ref, sin_ref):
    ang = pos_ref[...].astype(F32) * freq_ref[...]
    cos_ref[...] = jnp.cos(ang)
    sin_ref[...] = jnp.sin(ang)


def _rope_tables(positions, tm=1024):
    t = positions.size
    half = RET_DK // 2
    inv_freq = 1.0 / (RET_THETA_BASE ** jnp.linspace(0.0, 1.0, half, dtype=F32))
    return pl.pallas_call(
        _rope_kernel,
        grid=(t // tm,),
        in_specs=[pl.BlockSpec((tm, 1), lambda i: (i, 0)), pl.BlockSpec((1, half), lambda i: (0, 0))],
        out_specs=[pl.BlockSpec((tm, half), lambda i: (i, 0))] * 2,
        out_shape=[jax.ShapeDtypeStruct((t, half), F32)] * 2,
        compiler_params=_params(1),
        name="rope_tables",
    )(positions.reshape(t, 1), inv_freq.reshape(1, half))


def _retention_kernel(q_ref, k_ref, v_ref, z_ref, cos_ref, sin_ref, dintra_ref, dq_ref, dk_ref, dch_ref,
                      o_ref, state_ref, *, n_sub, n_heads):
    c = RET_CHUNK
    half = RET_DK // 2

    @pl.when(pl.program_id(2) == 0)
    def _():
        state_ref[...] = jnp.zeros_like(state_ref)

    def rotate(t, cosv, sinv):
        even, odd = t[:, :half], t[:, half:]
        return jnp.concatenate([even * cosv - odd * sinv, even * sinv + odd * cosv], axis=1)

    for j in range(n_sub):
        rows = pl.ds(j * c, c)
        cosv, sinv = cos_ref[rows, :], sin_ref[rows, :]
        for hh in range(n_heads):
            kcols = slice(hh * RET_DK, (hh + 1) * RET_DK)
            vcols = slice(hh * RET_DV, (hh + 1) * RET_DV)
            qr = rotate(q_ref[rows, kcols].astype(F32), cosv, sinv)
            kr = rotate(k_ref[rows, kcols].astype(F32), cosv, sinv)
            qb, kb = qr.astype(BF16), kr.astype(BF16)
            kdb = (kr * dk_ref[hh]).astype(BF16)
            v = v_ref[rows, vcols]
            state = state_ref[hh]
            scores = lax.dot_general(qb, kb, NT_DIMS, preferred_element_type=F32) * dintra_ref[hh]
            out = (jnp.dot(scores.astype(BF16), v, preferred_element_type=F32)
                   + jnp.dot(qb, state.astype(BF16), preferred_element_type=F32) * dq_ref[hh])
            state_ref[hh] = state * dch_ref[hh] + lax.dot_general(kdb, v, TN_DIMS, preferred_element_type=F32)
            mu = jnp.mean(out, axis=-1, keepdims=True)
            var = jnp.mean(jnp.square(out - mu), axis=-1, keepdims=True)
            normed = (out - mu) * lax.rsqrt(var + LN_EPS)
            o_ref[rows, vcols] = (normed * _silu(z_ref[rows, vcols].astype(F32))).astype(BF16)


def _retention_decays():
    h, c = RET_HEADS, RET_CHUNK
    k_scale = RET_DK ** -0.5
    assert math.frexp(k_scale)[0] == 0.5
    log_gamma = jnp.log1p(-jnp.exp2(-5.0 - jnp.arange(h, dtype=F32)))
    idx = jnp.arange(c, dtype=F32)
    rel = idx[:, None] - idx[None, :]
    dintra = jnp.where(rel >= 0, jnp.exp(log_gamma[:, None, None] * jnp.maximum(rel, 0.0)), 0.0) * k_scale
    dq = jnp.exp(log_gamma[:, None] * (idx + 1.0))
    dk = jnp.exp(log_gamma[:, None] * (c - 1.0 - idx)) * k_scale
    dch = jnp.exp(log_gamma * c)
    return (dintra,
            jnp.broadcast_to(dq[:, :, None], (h, c, RET_DV)),
            jnp.broadcast_to(dk[:, :, None], (h, c, RET_DK)),
            jnp.broadcast_to(dch[:, None, None], (h, 1, RET_DV)))


def _retention(qkvz, cos, sin, bsz, seq, *, ts=1024, hps=4):
    h, dk, dv, c = RET_HEADS, RET_DK, RET_DV, RET_CHUNK
    ns = seq // ts
    ng = h // hps
    v0 = 2 * h * dk // (hps * dv)
    dintra, dq, dkt, dch = _retention_decays()
    row = lambda b, hh, s: b * ns + s
    qk, vz = qkvz, qkvz
    return pl.pallas_call(
        functools.partial(_retention_kernel, n_sub=ts // c, n_heads=hps),
        grid=(bsz, ng, ns),
        in_specs=[pl.BlockSpec((ts, hps * dk), lambda b, hh, s: (row(b, hh, s), hh)),
                  pl.BlockSpec((ts, hps * dk), lambda b, hh, s: (row(b, hh, s), ng + hh)),
                  pl.BlockSpec((ts, hps * dv), lambda b, hh, s: (row(b, hh, s), v0 + hh)),
                  pl.BlockSpec((ts, hps * dv), lambda b, hh, s: (row(b, hh, s), v0 + ng + hh)),
                  pl.BlockSpec((ts, dk // 2), lambda b, hh, s: (row(b, hh, s), 0)),
                  pl.BlockSpec((ts, dk // 2), lambda b, hh, s: (row(b, hh, s), 0)),
                  pl.BlockSpec((hps, c, c), lambda b, hh, s: (hh, 0, 0)),
                  pl.BlockSpec((hps, c, dv), lambda b, hh, s: (hh, 0, 0)),
                  pl.BlockSpec((hps, c, dk), lambda b, hh, s: (hh, 0, 0)),
                  pl.BlockSpec((hps, 1, dv), lambda b, hh, s: (hh, 0, 0))],
        out_specs=pl.BlockSpec((ts, hps * dv), lambda b, hh, s: (row(b, hh, s), hh)),
        out_shape=jax.ShapeDtypeStruct((bsz * seq, h * dv), BF16),
        scratch_shapes=[pltpu.VMEM((hps, dk, dv), F32)],
        compiler_params=_params(3),
        name="retention",
    )(qk, qk, vz, vz, cos, sin, dintra, dq, dkt, dch)


def _deinterleave_matrix(width):
    p = np.zeros((width, width), np.float32)
    j = np.arange(width // 2)
    p[2 * j, j] = 1.0
    p[2 * j + 1, width // 2 + j] = 1.0
    return jnp.asarray(p, BF16)


def _proj_rows(xm):
    return 2048 if xm.dtype == BF16 else 1024


def _retention_layer(x, xm, cos, sin, w_in, w_out, gain, bias, bsz, seq):
    h, dk, dv = RET_HEADS, RET_DK, RET_DV
    tn, tm = 1024, _proj_rows(xm)
    n_qk = 2 * h * dk // tn
    qkvz = _proj(xm, w_in, n_tiles=n_qk + 2 * h * dv // tn, col_block0=0, tm=tm, tn=tn, out_dtype=BF16,
                 perm=_deinterleave_matrix(dk), perm_tiles=n_qk)
    mixed = _retention(qkvz, cos, sin, bsz, seq)
    return _out_ln(mixed, w_out.astype(BF16), x, gain, bias)


def _compress_kernel(*refs, transposed):
    x_refs, (pe_ref, w1_ref, w2_ref, o_ref) = refs[:-4], refs[-4:]
    n_blk = o_ref.shape[3] if transposed else o_ref.shape[2]
    half = CMP_LEN // 2
    lo, hi = [], []
    for l in range(half):
        rows = jnp.concatenate([x_ref[pl.ds(l, n_blk, stride=CMP_STRIDE), :] for x_ref in x_refs], axis=1)
        lo.append((rows + pe_ref[l:l + 1, :]).astype(BF16))
        hi.append((rows + pe_ref[half + l:half + l + 1, :]).astype(BF16))
    acc_lo = jnp.dot(jnp.concatenate(lo, axis=1), w1_ref[0], preferred_element_type=F32)
    acc_hi = jnp.dot(jnp.concatenate(hi, axis=1), w1_ref[1], preferred_element_type=F32)
    hid = acc_lo + pltpu.roll(acc_hi, shift=n_blk - 1, axis=0)
    out = jnp.dot(_silu(hid).astype(BF16), w2_ref[...], preferred_element_type=F32)
    keep = lax.broadcasted_iota(jnp.int32, out.shape, 0) < n_blk - 1
    out = jnp.where(keep, out, 0.0)
    if transposed:
        width = out.shape[1]
        o_ref[0, 0, :width, :] = out.T.astype(BF16)
        o_ref[0, 0, width:, :] = jnp.ones((o_ref.shape[2] - width, n_blk), BF16)
    else:
        o_ref[0, 0] = out.astype(BF16)


def _compress(kvc, pe, w1, w2, bsz, seq, *, width, col_block0, out_rows=None):
    g = NSA_GROUPS
    n_blk = seq // CMP_STRIDE
    n_slabs = width // LANES
    out_block = (1, 1, n_blk, w2.shape[1]) if out_rows is None else (1, 1, out_rows, n_blk)
    slab = lambda j: pl.BlockSpec((seq, LANES), lambda b, gg: (b, (col_block0 + gg) * n_slabs + j))
    return pl.pallas_call(
        functools.partial(_compress_kernel, transposed=out_rows is not None),
        grid=(bsz, g),
        in_specs=[slab(j) for j in range(n_slabs)] + [
                  pl.BlockSpec(pe.shape, lambda b, gg: (0, 0)),
                  pl.BlockSpec(w1.shape, lambda b, gg: (0, 0, 0)),
                  pl.BlockSpec(w2.shape, lambda b, gg: (0, 0))],
        out_specs=pl.BlockSpec(out_block, lambda b, gg: (b, gg, 0, 0)),
        out_shape=jax.ShapeDtypeStruct((bsz, g) + out_block[2:], BF16),
        compiler_params=_params(2),
        name="compress",
    )(*([kvc] * n_slabs), pe, w1, w2)


def _values_t_kernel(v_ref, o_ref):
    o_ref[0, 0, :NSA_DV, :] = v_ref[...].astype(F32).T.astype(BF16)
    o_ref[0, 0, NSA_DV:, :] = jnp.ones((V_ROWS - NSA_DV, v_ref.shape[0]), BF16)


def _values_t(kv, bsz, seq):
    g, dv, dkp = NSA_GROUPS, NSA_DV, NSA_DKP
    branch_blocks = g * (dkp + dv) // dv
    v0 = g * dkp // dv
    return pl.pallas_call(
        _values_t_kernel,
        grid=(bsz, 2 * g),
        in_specs=[pl.BlockSpec((seq, dv), lambda b, j: (b, v0 + (j // g) * branch_blocks + j % g))],
        out_specs=pl.BlockSpec((1, 1, V_ROWS, seq), lambda b, j: (b, j, 0, 0)),
        out_shape=jax.ShapeDtypeStruct((bsz, 2 * g, V_ROWS, seq), BF16),
        compiler_params=_params(2),
        name="values_t",
    )(kv)


def _pad_last(a, width):
    return jnp.pad(a, [(0, 0)] * (a.ndim - 1) + [(0, width - a.shape[-1])])


def _nsa_shared_kv(x, w_kv, pe_k, pe_v, w_ck1, w_ck2, w_cv1, w_cv2, bsz, seq):
    g, dk, dv, dkp = NSA_GROUPS, NSA_DK, NSA_DV, NSA_DKP
    tn = g * (dk + dv)
    pad = (g, dk, dkp)
    kvc = _proj(x, w_kv, n_tiles=1, col_block0=0, tm=1024, tn=tn, out_dtype=F32, head_pad=pad)
    kv = _proj(x, w_kv, n_tiles=2, col_block0=1, tm=1024, tn=tn, out_dtype=BF16, head_pad=pad)
    w1k = _pad_last(w_ck1.reshape(CMP_LEN, dk, CMP_HID).transpose(0, 2, 1), dkp).transpose(0, 2, 1).astype(BF16)
    w1k = w1k.reshape(2, CMP_LEN // 2 * dkp, CMP_HID)
    w1v = w_cv1.reshape(2, CMP_LEN // 2 * dv, CMP_HID).astype(BF16)
    kc = _compress(kvc, _pad_last(pe_k, dkp), w1k, _pad_last(w_ck2, dkp).astype(BF16), bsz, seq,
                   width=dkp, col_block0=0)
    vct = _compress(kvc, pe_v, w1v, w_cv2.astype(BF16), bsz, seq, width=dv, col_block0=g * dkp // dv,
                    out_rows=V_ROWS)
    return kc, vct, kv, _values_t(kv, bsz, seq)


def _nsa_attn_kernel(q_ref, zc_ref, zs_ref, zw_ref, x_ref, wg_ref, kc_ref, vct_ref, ovt_ref,
                     ks_ref, vst_ref, kw_ref, vwt_ref, o_ref,
                     bias_ref, sa_ref, sb_ref, m_ref, acc_ref, mix_ref, zsil_ref, wgb_ref, gates_ref, *, gps):
    tq, r, dv, dkp, tk = q_ref.shape[0], NSA_HPG, NSA_DV, NSA_DKP, SEL_TK
    lanes = r * tq
    n_blk = kc_ref.shape[2]
    n_sel = n_blk * CMP_STRIDE // SEL_LEN
    q0 = pl.program_id(2) * tq

    def qpos(shape):
        return q0 + lax.rem(lax.broadcasted_iota(jnp.int32, shape, 1), tq)

    def all_heads(tile):
        return jnp.concatenate([tile] * r, axis=1)

    def weights(s):
        return jnp.exp2(s - jnp.max(s, axis=0, keepdims=True)).astype(BF16)

    @pl.when((pl.program_id(0) == 0) & (pl.program_id(1) == 0) & (pl.program_id(2) == 0))
    def _():
        real = lax.broadcasted_iota(jnp.int32, wg_ref.shape, 0) < N_BRANCH * NSA_HEADS
        wgb_ref[...] = jnp.where(real, wg_ref[...], 0.0).astype(BF16)

    gates_ref[...] = _sigmoid(lax.dot_general(wgb_ref[...], x_ref[...], NT_DIMS, preferred_element_type=F32))

    def gated(gl, o_aug, inv_l, silu_z, br, h):
        cols = slice(h * tq, (h + 1) * tq)
        head = (pl.program_id(1) * gps + gl) * r + h
        row_scale = gates_ref[pl.ds(br * NSA_HEADS + head, 1), :] * inv_l[:, cols]
        return (o_aug[:dv, cols] * row_scale).T * silu_z

    q4s = []
    for gl in range(gps):
        q4 = jnp.concatenate([q_ref[:, (gl * r + h) * dkp:(gl * r + h + 1) * dkp] for h in range(r)], axis=0)
        q4s.append(q4)

        s = lax.dot_general(kc_ref[0, gl], q4, NT_DIMS, preferred_element_type=F32)
        blk_end = lax.broadcasted_iota(jnp.int32, (n_blk, tq), 0) * CMP_STRIDE + (CMP_LEN - 1)
        s = s + all_heads(jnp.where(blk_end <= qpos((n_blk, tq)), 0.0, NEG_INF))
        eb = weights(s)
        o_cmp = jnp.dot(vct_ref[0, gl], eb, preferred_element_type=F32)
        inv_c = jnp.where(qpos((1, lanes)) >= CMP_LEN - 1, 1.0 / o_cmp[dv:dv + 1, :], 0.0)

        imp = sum(jnp.dot(ovt_ref[...], eb[:, h * tq:(h + 1) * tq], preferred_element_type=F32)
                  * inv_c[:, h * tq:(h + 1) * tq] for h in range(r))
        imp = imp[:n_sel, :]
        blk = lax.broadcasted_iota(jnp.int32, imp.shape, 0)
        cur = (q0 + lax.broadcasted_iota(jnp.int32, imp.shape, 1)) // SEL_LEN
        forced = (blk == 0) | (blk == cur) | (blk == cur - 1)
        score = jnp.where(forced, FORCE_SCORE, jnp.where(blk <= cur, imp, -1.0))
        groups = [score[g0:g0 + SUBLANES, :] for g0 in range(0, n_sel, SUBLANES)]
        ranks = [jnp.zeros((SUBLANES, tq), F32) for _ in groups]
        for i in range(n_sel):
            row = score[i:i + 1, :]
            for gi, grp in enumerate(groups):
                g0 = gi * SUBLANES
                if g0 > i:
                    ahead = row >= grp
                elif g0 + SUBLANES - 1 < i:
                    ahead = row > grp
                else:
                    after = lax.broadcasted_iota(jnp.int32, grp.shape, 0) + g0 > i
                    ahead = (row > grp) | (after & (row == grp))
                ranks[gi] = ranks[gi] + jnp.where(ahead, 1.0, 0.0)
        rank = jnp.concatenate(ranks, axis=0)
        bias_ref[gl] = jnp.where(rank < float(min(SEL_TOPK, n_sel)), 0.0, NEG_INF)

        span = WINDOW + tq
        k0 = pl.multiple_of(jnp.maximum(q0 - WINDOW, 0), tq)
        s = lax.dot_general(kw_ref[pl.ds(k0, span), gl * dkp:(gl + 1) * dkp], q4, NT_DIMS,
                            preferred_element_type=F32)
        dist = qpos((span, tq)) - (k0 + lax.broadcasted_iota(jnp.int32, (span, tq), 0))
        s = s + all_heads(jnp.where((dist >= 0) & (dist < WINDOW), 0.0, NEG_INF))
        o_win = jnp.dot(vwt_ref[0, gl, :, pl.ds(k0, span)], weights(s), preferred_element_type=F32)
        inv_w = 1.0 / o_win[dv:dv + 1, :]

        for h in range(r):
            cols = slice((gl * r + h) * dv, (gl * r + h + 1) * dv)
            mix_ref[:, cols] = (gated(gl, o_cmp, inv_c, _silu(zc_ref[:, cols].astype(F32)), 0, h)
                                + gated(gl, o_win, inv_w, _silu(zw_ref[:, cols].astype(F32)), 2, h))
    zsil_ref[...] = _silu(zs_ref[...].astype(F32))

    m_ref[...] = jnp.full(m_ref.shape, NEG_INF, F32)
    acc_ref[...] = jnp.zeros_like(acc_ref)
    blocks_per_tile = tk // SEL_LEN
    n_tiles = q0 // tk + 1
    causal_gap = (q0 + lax.broadcasted_iota(jnp.int32, (tk, tq), 1)) - lax.broadcasted_iota(jnp.int32, (tk, tq), 0)

    def scores(gl, tile):
        kbase = pl.multiple_of(tile * tk, tk)
        rows = bias_ref[gl, pl.ds(pl.multiple_of(tile * blocks_per_tile, blocks_per_tile), blocks_per_tile), :]
        bias = jnp.concatenate([jnp.broadcast_to(rows[j:j + 1, :], (SEL_LEN, tq)) for j in range(blocks_per_tile)],
                               axis=0)
        bias = jnp.where(causal_gap >= kbase, bias, NEG_INF)
        return (lax.dot_general(ks_ref[pl.ds(kbase, tk), gl * dkp:(gl + 1) * dkp], q4s[gl], NT_DIMS,
                                preferred_element_type=F32) + all_heads(bias))

    def attend(gl, s_ref, tile):
        kbase = pl.multiple_of(tile * tk, tk)
        s = s_ref[gl]
        m_prev = m_ref[gl]
        m_new = jnp.maximum(m_prev, jnp.max(s, axis=0, keepdims=True))
        pv = jnp.dot(vst_ref[0, gl, :, pl.ds(kbase, tk)], jnp.exp2(s - m_new).astype(BF16),
                     preferred_element_type=F32)
        acc_ref[gl] = jnp.exp2(m_prev - m_new) * acc_ref[gl] + pv
        m_ref[gl] = m_new

    for gl in range(gps):
        sa_ref[gl] = scores(gl, 0)

    def pair(jp, carry):
        first = 2 * jp
        for gl in range(gps):
            sb_ref[gl] = scores(gl, first + 1)
            attend(gl, sa_ref, first)
        for gl in range(gps):
            sa_ref[gl] = scores(gl, jnp.minimum(first + 2, n_tiles - 1))
            attend(gl, sb_ref, first + 1)
        return carry

    lax.fori_loop(0, n_tiles // 2, pair, 0)

    @pl.when(lax.rem(n_tiles, 2) == 1)
    def _():
        for gl in range(gps):
            attend(gl, sa_ref, n_tiles - 1)

    for gl in range(gps):
        o_sel = acc_ref[gl]
        inv_s = 1.0 / o_sel[dv:dv + 1, :]
        for h in range(r):
            cols = slice((gl * r + h) * dv, (gl * r + h + 1) * dv)
            o_ref[:, cols] = (mix_ref[:, cols] + gated(gl, o_sel, inv_s, zsil_ref[:, cols], 1, h)).astype(BF16)


def _overlap_matrix_t(n_blk, n_sel):
    a, b = SEL_LEN // CMP_STRIDE, CMP_LEN // CMP_STRIDE
    span = a + b - 2
    n = np.arange(n_blk)[None, :]
    j = np.arange(LANES)[:, None]
    diff = n - a * j
    ov = np.where((diff >= 0) & (diff <= span), np.minimum(np.minimum(diff, span - diff), min(a, b) - 1) + 1, 0)
    ov = np.where((n < n_blk - 1) & (j < n_sel), ov, 0)
    return jnp.asarray(ov, BF16)


def _nsa_attention(qp, zp, xm, w_t, gate_block, kc, vct, kv, vt, bsz, seq, *, gps=ATT_GPS):
    g, r, dv, dkp, tq, tk = NSA_GROUPS, NSA_HPG, NSA_DV, NSA_DKP, ATT_TQ, SEL_TK
    d = xm.shape[1]
    ng = g // gps
    nq = seq // tq
    n_blk = seq // CMP_STRIDE
    n_sel = seq // SEL_LEN
    ovt = _overlap_matrix_t(n_blk, n_sel)
    row = lambda b, gg, i: b * nq + i
    z_spec = lambda br: pl.BlockSpec((tq, gps * r * dv), lambda b, gg, i: (row(b, gg, i), br * ng + gg))
    kw0 = g * (dkp + dv) // (gps * dkp)
    return pl.pallas_call(
        functools.partial(_nsa_attn_kernel, gps=gps),
        grid=(bsz, ng, nq),
        in_specs=[pl.BlockSpec((tq, gps * r * dkp), lambda b, gg, i: (row(b, gg, i), gg)),
                  z_spec(0), z_spec(1), z_spec(2),
                  pl.BlockSpec((tq, d), lambda b, gg, i: (row(b, gg, i), 0)),
                  pl.BlockSpec((LANES, d), lambda b, gg, i: (gate_block, 0)),
                  pl.BlockSpec((1, gps, n_blk, dkp), lambda b, gg, i: (b, gg, 0, 0)),
                  pl.BlockSpec((1, gps, V_ROWS, n_blk), lambda b, gg, i: (b, gg, 0, 0)),
                  pl.BlockSpec(ovt.shape, lambda b, gg, i: (0, 0)),
                  pl.BlockSpec((seq, gps * dkp), lambda b, gg, i: (b, gg)),
                  pl.BlockSpec((1, gps, V_ROWS, seq), lambda b, gg, i: (b, gg, 0, 0)),
                  pl.BlockSpec((seq, gps * dkp), lambda b, gg, i: (b, kw0 + gg)),
                  pl.BlockSpec((1, gps, V_ROWS, seq), lambda b, gg, i: (b, ng + gg, 0, 0))],
        out_specs=pl.BlockSpec((tq, gps * r * dv), lambda b, gg, i: (row(b, gg, i), gg)),
        out_shape=jax.ShapeDtypeStruct((bsz * seq, g * r * dv), BF16),
        scratch_shapes=[pltpu.VMEM((gps, n_sel, tq), F32),
                        pltpu.VMEM((gps, tk, r * tq), F32),
                        pltpu.VMEM((gps, tk, r * tq), F32),
                        pltpu.VMEM((gps, 1, r * tq), F32),
                        pltpu.VMEM((gps, V_ROWS, r * tq), F32),
                        pltpu.VMEM((tq, gps * r * dv), F32),
                        pltpu.VMEM((tq, gps * r * dv), F32),
                        pltpu.VMEM((LANES, d), BF16),
                        pltpu.VMEM((LANES, tq), F32)],
        compiler_params=_params(3),
        name="nsa_attn",
    )(qp, zp, zp, zp, xm, w_t, kc, vct, ovt, kv, vt, kv, vt)


def _nsa_layer(x, xm, shared, w_in, w_out, gain, bias, bsz, seq, *, last):
    kc, vct, kv, vt = shared
    h, g, r, dk, dv, dkp = NSA_HEADS, NSA_GROUPS, NSA_HPG, NSA_DK, NSA_DV, NSA_DKP
    nq = h * dk
    nz = N_BRANCH * h * dv
    tn, tm = 1024, _proj_rows(xm)
    w_t = w_in.T
    qp = _proj(xm, w_t, n_tiles=g, col_block0=0, tm=tm, tn=r * dk, out_dtype=BF16, w_transposed=True,
               head_pad=(r, dk, dkp), out_scale=(dk ** -0.5) * math.log2(math.e))
    zp = _proj(xm, w_t, n_tiles=nz // tn, col_block0=nq // tn, tm=tm, tn=tn, out_dtype=BF16, w_transposed=True)
    mixed = _nsa_attention(qp, zp, xm, w_t, (nq + nz) // LANES, kc, vct, kv, vt, bsz, seq)
    return _out_ln(mixed, w_out.astype(BF16), x, gain, bias, with_bf16=not last)


def kernel(x, positions, ret_w_in_0, ret_w_out_0, ln_g_0, ln_b_0, ret_w_in_1, ret_w_out_1, ln_g_1, ln_b_1,
           nsa_w_kv, nsa_pe_k, nsa_pe_v, nsa_w_ck1, nsa_w_ck2, nsa_w_cv1, nsa_w_cv2,
           nsa_w_in_2, nsa_w_out_2, ln_g_2, ln_b_2, nsa_w_in_3, nsa_w_out_3, ln_g_3, ln_b_3):
    bsz, seq, d = x.shape
    h = x.reshape(bsz * seq, d)
    cos, sin = _rope_tables(positions)
    h, hm = _retention_layer(h, h, cos, sin, ret_w_in_0, ret_w_out_0, ln_g_0, ln_b_0, bsz, seq)
    h, hm = _retention_layer(h, hm, cos, sin, ret_w_in_1, ret_w_out_1, ln_g_1, ln_b_1, bsz, seq)
    shared = _nsa_shared_kv(hm, nsa_w_kv, nsa_pe_k, nsa_pe_v, nsa_w_ck1, nsa_w_ck2, nsa_w_cv1, nsa_w_cv2, bsz, seq)
    h, hm = _nsa_layer(h, hm, shared, nsa_w_in_2, nsa_w_out_2, ln_g_2, ln_b_2, bsz, seq, last=False)
    h = _nsa_layer(h, hm, shared, nsa_w_in_3, nsa_w_out_3, ln_g_3, ln_b_3, bsz, seq, last=True)
    return h.reshape(bsz, seq, d)
```
